```python
import math
import jax, jax.numpy as jnp
from jax import lax
import numpy as np

D_MODEL = 2048
BATCH = 2
SEQ = 4096
DEPTH = 4
DEC_BATCH = 128
DEC_SEQ = 1
PAST_LEN = 8192
PAGE_SIZE = 128

N_HEADS_A = 8
NOPE_DIM = 128
ROPE_DIM = 64
QK_DIM = NOPE_DIM + ROPE_DIM
V_DIM = 128
Q_LORA = D_MODEL // 4
KV_LORA = D_MODEL // 8
ROPE_THETA = 10000.0
Q_BLOCK = 128
D_SSM = D_MODEL // 4
SSM_HEAD_DIM = 64
N_HEADS_S = D_SSM // SSM_HEAD_DIM
N_GROUPS_S = 2
D_STATE = 128
SSM_CONV = 4
SSD_CHUNK = 128
CONV_DIM = D_SSM + 2 * N_GROUPS_S * D_STATE
D_CM = D_MODEL // 4
CHUNK = 128
N_CG = 4
CG_DIM = D_CM // N_CG
D_MIX = N_HEADS_A * V_DIM + D_SSM + D_CM
D_FF = ((8 * D_MODEL // 3 + 255) // 256) * 256
FFN_CONV = 3
EPS = 1e-6

kernel_name = "hymba_mla_ssd_chunkmlp_convffn_step"


def rmsnorm(x, g):
    xf = x.astype(jnp.float32)
    y = xf * lax.rsqrt(jnp.mean(xf * xf, axis=-1, keepdims=True) + EPS)
    return (y * g.astype(jnp.float32)).astype(x.dtype)


def rope_cos_sin(pos):
    inv = ROPE_THETA ** (-jnp.arange(0, ROPE_DIM, 2, dtype=jnp.float32) / ROPE_DIM)
    ang = pos.astype(jnp.float32)[:, None] * inv[None, :]
    return jnp.cos(ang), jnp.sin(ang)


def apply_rope(x, cos, sin):
    x1, x2 = jnp.split(x, 2, axis=-1)
    cos = cos.astype(x.dtype)
    sin = sin.astype(x.dtype)
    return jnp.concatenate([x1 * cos - x2 * sin, x1 * sin + x2 * cos], axis=-1)


def causal_dwconv(x, prev, w, b):
    width = w.shape[0]
    t = x.shape[1]
    xp = jnp.concatenate([prev.astype(x.dtype), x], axis=1)
    y = xp[:, 0:t] * w[0]
    for i in range(1, width):
        y = y + xp[:, i:i + t] * w[i]
    return y + b, xp[:, t:]


def split_in(proj):
    sizes = (Q_LORA, KV_LORA, ROPE_DIM, D_SSM, CONV_DIM, N_HEADS_S, D_CM, D_CM)
    offs = [int(o) for o in np.cumsum(sizes)[:-1]]
    return jnp.split(proj, offs, axis=-1)


def mla_query(c_q, g_q_a, w_uq, g_qk_q, cos, sin):
    q = jnp.einsum('btr,rhd->bthd', rmsnorm(c_q, g_q_a), w_uq)
    q_rope = apply_rope(q[..., NOPE_DIM:], cos[:, None, :], sin[:, None, :])
    return rmsnorm(jnp.concatenate([q[..., :NOPE_DIM], q_rope], axis=-1), g_qk_q)


def mla_kv(c_kv, k_rope, w_uk, w_uv, g_k):
    k_nope = jnp.einsum('...tc,chd->...thd', c_kv, w_uk)
    k_r = jnp.broadcast_to(k_rope[..., None, :], k_nope.shape[:-1] + (ROPE_DIM,))
    k = rmsnorm(jnp.concatenate([k_nope, k_r], axis=-1), g_k)
    v = jnp.einsum('...tc,chd->...thd', c_kv, w_uv)
    return k, v


def causal_block_attention(q, k, v):
    b, s, h, dk = q.shape
    nb = s // Q_BLOCK
    scale = QK_DIM ** -0.5
    qb = q.reshape(b, nb, Q_BLOCK, h, dk).transpose(1, 0, 2, 3, 4)
    kpos = jnp.arange(s)

    def one_block(args):
        qi, i = args
        sc = jnp.einsum('bqhd,bkhd->bhqk', qi, k).astype(jnp.float32) * scale
        qpos = i * Q_BLOCK + jnp.arange(Q_BLOCK)
        sc = jnp.where(kpos[None, :] <= qpos[:, None], sc, -jnp.inf)
        p = jax.nn.softmax(sc, axis=-1).astype(v.dtype)
        return jnp.einsum('bhqk,bkhd->bqhd', p, v)

    out = lax.map(one_block, (qb, jnp.arange(nb)))
    return out.transpose(1, 0, 2, 3, 4).reshape(b, s, h, V_DIM)


def paged_mla_decode(q, c_new, kr_new, cache_lat, cache_kr, li, page_table, w_uk, w_uv, g_k):
    bd, t = q.shape[:2]
    scale = QK_DIM ** -0.5

    def scores(k):
        return jnp.einsum('bthd,bphd->bhtp', q, k).astype(jnp.float32) * scale

    def merge(carry, s, v):
        m, l, acc = carry
        m_new = jnp.maximum(m, jnp.max(s, axis=-1))
        corr = jnp.exp(m - m_new)
        p = jnp.exp(s - m_new[..., None])
        acc = acc * corr[..., None] + jnp.einsum('bhtp,bphd->bhtd', p, v.astype(jnp.float32))
        return (m_new, l * corr + jnp.sum(p, axis=-1), acc)

    def page_step(carry, page_ids):
        k, v = mla_kv(cache_lat[li, page_ids], cache_kr[li, page_ids], w_uk, w_uv, g_k)
        return merge(carry, scores(k), v), None

    init = (jnp.full((bd, N_HEADS_A, t), -jnp.inf, jnp.float32),
            jnp.zeros((bd, N_HEADS_A, t), jnp.float32),
            jnp.zeros((bd, N_HEADS_A, t, V_DIM), jnp.float32))
    carry, _ = lax.scan(page_step, init, page_table.T)
    k_n, v_n = mla_kv(c_new, kr_new, w_uk, w_uv, g_k)
    s_new = jnp.where(jnp.tril(jnp.ones((t, t), bool)), scores(k_n), -jnp.inf)
    _, l, acc = merge(carry, s_new, v_n)
    out = acc / l[..., None]
    return out.transpose(0, 2, 1, 3).astype(q.dtype)


def segsum(a):
    t = a.shape[-1]
    ar = jnp.broadcast_to(a[..., None], a.shape + (t,))
    ar = jnp.where(jnp.tril(jnp.ones((t, t), bool), -1), ar, 0.0)
    ss = jnp.cumsum(ar, axis=-2)
    return jnp.where(jnp.tril(jnp.ones((t, t), bool)), ss, -jnp.inf)


def ssd_inputs(xbc, dt_raw, dt_bias, a_log):
    x, bm, cm = jnp.split(xbc, [D_SSM, D_SSM + N_GROUPS_S * D_STATE], axis=-1)
    lead = x.shape[:-1]
    hpg = N_HEADS_S // N_GROUPS_S
    x = x.reshape(lead + (N_HEADS_S, SSM_HEAD_DIM))
    bm = jnp.repeat(bm.reshape(lead + (N_GROUPS_S, D_STATE)), hpg, axis=-2)
    cm = jnp.repeat(cm.reshape(lead + (N_GROUPS_S, D_STATE)), hpg, axis=-2)
    dt = jax.nn.softplus((dt_raw + dt_bias).astype(jnp.float32))
    a = -jnp.exp(a_log.astype(jnp.float32))
    return x, bm, cm, dt, dt * a


def ssd_chunked(xdt, da, bm, cm):
    b, s, h, p = xdt.shape
    c = s // SSD_CHUNK
    X = xdt.reshape(b, c, SSD_CHUNK, h, p)
    Bc = bm.reshape(b, c, SSD_CHUNK, h, D_STATE)
    Cc = cm.reshape(b, c, SSD_CHUNK, h, D_STATE)
    A = da.reshape(b, c, SSD_CHUNK, h).transpose(0, 3, 1, 2)
    A_cs = jnp.cumsum(A, axis=-1)
    Lmat = jnp.exp(segsum(A))
    sc = jnp.einsum('bclhn,bcshn->bhcls', Cc, Bc) * Lmat
    y_diag = jnp.einsum('bhcls,bcshp->bclhp', sc, X)
    decay_states = jnp.exp(A_cs[..., -1:] - A_cs).transpose(0, 2, 3, 1)
    states = jnp.einsum('bclhn,bclhp->bchpn', Bc * decay_states[..., None], X)
    states = jnp.concatenate([jnp.zeros_like(states[:, :1]), states], axis=1)
    decay_chunk = jnp.exp(segsum(jnp.pad(A_cs[..., -1], ((0, 0), (0, 0), (1, 0)))))
    states = jnp.einsum('bhzc,bchpn->bzhpn', decay_chunk, states)
    y_off = jnp.einsum('bclhn,bchpn->bclhp', Cc, states[:, :-1]) * jnp.exp(A_cs).transpose(0, 2, 3, 1)[..., None]
    return (y_diag + y_off).reshape(b, s, h, p), states[:, -1].astype(xdt.dtype)


def ssd_recurrent(h0, xdt, da, bm, cm):
    def step(hs, inp):
        x_t, a_t, b_t, c_t = inp
        hs = hs * jnp.exp(a_t)[..., None, None] + jnp.einsum('bhp,bhn->bhpn', x_t, b_t)
        return hs, jnp.einsum('bhpn,bhn->bhp', hs, c_t)
    seqs = (jnp.swapaxes(xdt, 0, 1), jnp.swapaxes(da, 0, 1), jnp.swapaxes(bm, 0, 1), jnp.swapaxes(cm, 0, 1))
    h_last, ys = lax.scan(step, h0.astype(jnp.float32), seqs)
    return jnp.swapaxes(ys, 0, 1), h_last.astype(h0.dtype)


def ssd_gate_out(y, x, d_skip, z, g):
    y = y + x * d_skip[:, None].astype(x.dtype)
    y = y.reshape(y.shape[:-2] + (D_SSM,)).astype(z.dtype)
    return rmsnorm(y * jax.nn.silu(z), g)


def spatial_gate(u, v, g, w_s, b_s):
    bsz, t = v.shape[:2]
    ln = min(t, CHUNK)
    vn = rmsnorm(v, g)
    vr = vn.reshape(bsz, t // ln, ln, N_CG, CG_DIM)
    w = jnp.tril(w_s[:, :ln, :ln])
    mixed = jnp.einsum('gts,bcsgd->bctgd', w, vr) + b_s[:, :ln].T[None, None, :, :, None]
    return u * mixed.reshape(bsz, t, D_CM), vn


def conv_ffn(h, prev, w_gu, cw, cb, w_down):
    gate, up = jnp.split(h @ w_gu, 2, axis=-1)
    gate, buf = causal_dwconv(gate, prev, cw, cb)
    return (jax.nn.silu(gate) * up) @ w_down, buf


def trunk_layer(x, pos, lw, attend, ssm_scan, conv_prev, ffn_prev):
    bsz, t = x.shape[:2]
    h = rmsnorm(x, lw['g_mix'])
    c_q, c_kv, k_r, z, xbc, dt_raw, u, v = split_in(h @ lw['w_in'])
    cos, sin = rope_cos_sin(pos)
    q = mla_query(c_q, lw['g_q_a'], lw['w_uq'], lw['g_qk_q'], cos, sin)
    c_kv = rmsnorm(c_kv, lw['g_kv_a'])
    k_r = apply_rope(k_r, cos, sin)
    att = attend(q, c_kv, k_r).reshape(bsz, t, N_HEADS_A * V_DIM)
    xbc_c, conv_buf = causal_dwconv(xbc, conv_prev, lw['ssm_conv_w'], lw['ssm_conv_b'])
    xs, bm, cm, dt, da = ssd_inputs(jax.nn.silu(xbc_c), dt_raw, lw['ssm_dt_bias'], lw['ssm_a_log'])
    ys, ssm_state = ssm_scan(xs * dt[..., None], da, bm, cm)
    ssm_out = ssd_gate_out(ys, xs, lw['ssm_d'], z, lw['g_ssm_out'])
    cm_out, v_rows = spatial_gate(u, v, lw['g_sgu'], lw['w_spatial'], lw['b_spatial'])
    mix = jnp.concatenate([att, ssm_out.astype(att.dtype), cm_out], axis=-1)
    x = x + mix @ lw['w_out']
    f, ffn_buf = conv_ffn(rmsnorm(x, lw['g_ffn']), ffn_prev, lw['w_gate_up'], lw['ffn_conv_w'], lw['ffn_conv_b'], lw['w_down'])
    return x + f, (c_kv, k_r, ssm_state, conv_buf, ffn_buf, v_rows)


def setup_inputs(seed: int = 0) -> dict:
    key = jax.random.key(seed)
    ks = list(jax.random.split(key, 40))
    f32 = jnp.float32

    def nrm(i, shape, scale):
        return scale * jax.random.normal(ks[i], shape, f32)

    def gain(i, n):
        return 1.0 + 0.02 * jax.random.normal(ks[i], (DEPTH, n), f32)

    n_pages = PAST_LEN // PAGE_SIZE
    n_used = DEC_BATCH * n_pages
    n_pool = n_used + n_used // 4
    page_table = jax.random.permutation(ks[0], n_pool)[:n_used].reshape(DEC_BATCH, n_pages).astype(jnp.int32)
    in_dim = Q_LORA + KV_LORA + ROPE_DIM + D_SSM + CONV_DIM + N_HEADS_S + 2 * D_CM
    dt0 = jnp.exp(jax.random.uniform(ks[1], (DEPTH, N_HEADS_S), f32, math.log(1e-3), math.log(1e-1)))
    dt_bias = dt0 + jnp.log(-jnp.expm1(-dt0))
    a_log = jnp.log(jax.random.uniform(ks[2], (DEPTH, N_HEADS_S), f32, 1.0, 16.0))
    return {
        'x_prompt': nrm(3, (BATCH, SEQ, D_MODEL), 1.0),
        'x_sample': nrm(4, (DEC_BATCH, DEC_SEQ, D_MODEL), 1.0),
        'cache_latent': nrm(5, (DEPTH, n_pool, PAGE_SIZE, KV_LORA), 1.0),
        'cache_krope': nrm(6, (DEPTH, n_pool, PAGE_SIZE, ROPE_DIM), 1.0),
        'state_ssm': nrm(7, (DEPTH, DEC_BATCH, N_HEADS_S, SSM_HEAD_DIM, D_STATE), 0.5),
        'state_conv': nrm(8, (DEPTH, DEC_BATCH, SSM_CONV - 1, CONV_DIM), 1.0),
        'state_ffn_conv': nrm(9, (DEPTH, DEC_BATCH, FFN_CONV - 1, D_FF), 1.0),
        'page_table': page_table,
        'g_mix': gain(10, D_MODEL),
        'w_in': nrm(11, (DEPTH, D_MODEL, in_dim), D_MODEL ** -0.5),
        'g_q_a': gain(12, Q_LORA),
        'w_uq': nrm(13, (DEPTH, Q_LORA, N_HEADS_A, QK_DIM), Q_LORA ** -0.5),
        'g_kv_a': gain(14, KV_LORA),
        'w_uk': nrm(15, (DEPTH, KV_LORA, N_HEADS_A, NOPE_DIM), KV_LORA ** -0.5),
        'w_uv': nrm(16, (DEPTH, KV_LORA, N_HEADS_A, V_DIM), KV_LORA ** -0.5),
        'g_qk_q': gain(17, QK_DIM),
        'g_qk_k': gain(18, QK_DIM),
        'ssm_conv_w': nrm(19, (DEPTH, SSM_CONV, CONV_DIM), SSM_CONV ** -0.5),
        'ssm_conv_b': nrm(20, (DEPTH, CONV_DIM), 0.01),
        'ssm_dt_bias': dt_bias,
        'ssm_a_log': a_log,
        'ssm_d': 1.0 + 0.1 * jax.random.normal(ks[21], (DEPTH, N_HEADS_S), f32),
        'g_ssm_out': gain(22, D_SSM),
        'g_sgu': gain(23, D_CM),
        'w_spatial': nrm(24, (DEPTH, N_CG, CHUNK, CHUNK), CHUNK ** -0.5),
        'b_spatial': 1.0 + 0.01 * jax.random.normal(ks[25], (DEPTH, N_CG, CHUNK), f32),
        'w_out': nrm(26, (DEPTH, D_MIX, D_MODEL), D_MIX ** -0.5),
        'g_ffn': gain(27, D_MODEL),
        'w_gate_up': nrm(28, (DEPTH, D_MODEL, 2 * D_FF), D_MODEL ** -0.5),
        'ffn_conv_w': nrm(29, (DEPTH, FFN_CONV, D_FF), FFN_CONV ** -0.5),
        'ffn_conv_b': nrm(30, (DEPTH, D_FF), 0.01),
        'w_down': nrm(31, (DEPTH, D_FF, D_MODEL), D_FF ** -0.5),
    }


def reference(x_prompt, x_sample, cache_latent, cache_krope, state_ssm, state_conv, state_ffn_conv, page_table,
              g_mix, w_in, g_q_a, w_uq, g_kv_a, w_uk, w_uv, g_qk_q, g_qk_k, ssm_conv_w, ssm_conv_b,
              ssm_dt_bias, ssm_a_log, ssm_d, g_ssm_out, g_sgu, w_spatial, b_spatial, w_out, g_ffn,
              w_gate_up, ffn_conv_w, ffn_conv_b, w_down):
    pos_p = jnp.arange(x_prompt.shape[1], dtype=jnp.int32)
    pos_s = PAST_LEN + jnp.arange(x_sample.shape[1], dtype=jnp.int32)
    bp = x_prompt.shape[0]
    yp, ys = x_prompt, x_sample
    st_p, st_s = [], []
    for li in range(DEPTH):
        lw = {
            'g_mix': g_mix[li], 'w_in': w_in[li], 'g_q_a': g_q_a[li], 'w_uq': w_uq[li],
            'g_kv_a': g_kv_a[li], 'w_uk': w_uk[li], 'w_uv': w_uv[li], 'g_qk_q': g_qk_q[li],
            'g_qk_k': g_qk_k[li], 'ssm_conv_w': ssm_conv_w[li], 'ssm_conv_b': ssm_conv_b[li],
            'ssm_dt_bias': ssm_dt_bias[li], 'ssm_a_log': ssm_a_log[li], 'ssm_d': ssm_d[li],
            'g_ssm_out': g_ssm_out[li], 'g_sgu': g_sgu[li], 'w_spatial': w_spatial[li],
            'b_spatial': b_spatial[li], 'w_out': w_out[li], 'g_ffn': g_ffn[li],
            'w_gate_up': w_gate_up[li], 'ffn_conv_w': ffn_conv_w[li], 'ffn_conv_b': ffn_conv_b[li],
            'w_down': w_down[li],
        }
        attend_p = lambda q, c, kr, lw=lw: causal_block_attention(q, *mla_kv(c, kr, lw['w_uk'], lw['w_uv'], lw['g_qk_k']))
        yp, sp = trunk_layer(yp, pos_p, lw, attend_p, ssd_chunked,
                             jnp.zeros((bp, SSM_CONV - 1, CONV_DIM), x_prompt.dtype),
                             jnp.zeros((bp, FFN_CONV - 1, D_FF), x_prompt.dtype))
        st_p.append(sp)
        attend_s = lambda q, c, kr, lw=lw, li=li: paged_mla_decode(q, c, kr, cache_latent, cache_krope, li, page_table,
                                                                 lw['w_uk'], lw['w_uv'], lw['g_qk_k'])
        scan_s = lambda xdt, da, bm, cm, li=li: ssd_recurrent(state_ssm[li], xdt, da, bm, cm)
        ys, ss = trunk_layer(ys, pos_s, lw, attend_s, scan_s, state_conv[li], state_ffn_conv[li])
        st_s.append(ss)
    p_latent = jnp.stack([s[0] for s in st_p])
    p_krope = jnp.stack([s[1] for s in st_p])
    p_ssm = jnp.stack([s[2] for s in st_p])
    p_conv = jnp.stack([s[3] for s in st_p])
    p_ffn_conv = jnp.stack([s[4] for s in st_p])
    s_latent = jnp.stack([s[0] for s in st_s])
    s_krope = jnp.stack([s[1] for s in st_s])
    s_ssm = jnp.stack([s[2] for s in st_s])
    s_conv = jnp.stack([s[3] for s in st_s])
    s_ffn_conv = jnp.stack([s[4] for s in st_s])
    s_chunk_v = jnp.stack([s[5] for s in st_s])
    return (yp, ys, p_latent, p_krope, p_ssm, p_conv, p_ffn_conv, s_latent, s_krope, s_ssm, s_conv, s_ffn_conv, s_chunk_v)
```

```python
import functools

import numpy as np
import jax
import jax.numpy as jnp
from jax import lax
from jax.experimental import pallas as pl
from jax.experimental.pallas import tpu as pltpu

F32 = jnp.float32
BF16 = jnp.bfloat16
EPS = 1e-6
HIGHEST = lax.Precision.HIGHEST

D_MODEL = 2048
N_HEADS_A = 8
NOPE_DIM = 128
ROPE_DIM = 64
QK_DIM = NOPE_DIM + ROPE_DIM
QK_PAD = 256
V_DIM = 128
Q_LORA = 512
KV_LORA = 256
ROPE_THETA = 10000.0
D_SSM = 512
SSM_HEAD_DIM = 64
N_HEADS_S = 8
N_GROUPS_S = 2
D_STATE = 128
SSM_CONV = 4
SSD_CHUNK = 128
CONV_DIM = 1024
D_CM = 512
CHUNK = 128
N_CG = 4
CG_DIM = 128
D_FF = 5632
FFN_CONV = 3
PAGE = 128

COL_CQ, COL_Z, COL_U, COL_V, COL_XBC, COL_CKV, COL_KR, COL_DT = 0, 512, 1024, 1536, 2048, 3072, 3328, 3456
IN_PAD = 3584

LANE = 128
VMEM_LIMIT = 56 * 1024 * 1024

NT_DIMS = (((1,), (1,)), ((), ()))


def _cparams(sem):
    return pltpu.CompilerParams(dimension_semantics=sem, vmem_limit_bytes=VMEM_LIMIT)


def _rms(x, g):
    return x * lax.rsqrt(jnp.mean(x * x, axis=-1, keepdims=True) + EPS) * g


def _silu(x):
    return x * (1.0 / (1.0 + jnp.exp(-x)))


def _softplus(x):
    return jnp.maximum(x, 0.0) + jnp.log(1.0 + jnp.exp(-jnp.abs(x)))


def _dot(a, b):
    return jnp.dot(a, b, preferred_element_type=F32)


def _dot_nt(a, b):
    return lax.dot_general(a, b, NT_DIMS, preferred_element_type=F32)


def _norm_matmul_kernel(x_ref, g_ref, w_ref, o_ref, h_ref):
    @pl.when(pl.program_id(1) == 0)
    def _():
        h_ref[...] = _rms(x_ref[...], g_ref[...]).astype(BF16)

    o_ref[...] = _dot(h_ref[...], w_ref[...])


def norm_matmul(x, g, w, *, tm, tn):
    r, k = x.shape
    n = w.shape[1]
    return pl.pallas_call(
        _norm_matmul_kernel,
        out_shape=jax.ShapeDtypeStruct((r, n), F32),
        grid=(r // tm, n // tn),
        in_specs=[pl.BlockSpec((tm, k), lambda i, j: (i, 0)),
                  pl.BlockSpec((1, k), lambda i, j: (0, 0)),
                  pl.BlockSpec((k, tn), lambda i, j: (0, j))],
        out_specs=pl.BlockSpec((tm, tn), lambda i, j: (i, j)),
        scratch_shapes=[pltpu.VMEM((tm, k), BF16)],
        compiler_params=_cparams(("parallel", "arbitrary")),
        name="norm_matmul",
    )(x, g, w)


def _mla_prep_kernel(cq_ref, ckv_ref, kr_ref, gqa_ref, gkva_ref, wuq_ref, wuk_ref, wuv_ref, gq_ref, gk_ref,
                     cos_ref, s1_ref, s2_ref, q_ref, k_ref, v_ref, lat_ref, kro_ref):
    cos = cos_ref[...]
    s1 = s1_ref[...]
    s2 = s2_ref[...]

    def rope(b):
        return b * cos + pltpu.roll(b, 96, 1) * s1 + pltpu.roll(b, 32, 1) * s2

    def head_norm(a, b, g):
        ss = jnp.sum(a * a + b * b, axis=-1, keepdims=True)
        inv = lax.rsqrt(ss * (1.0 / QK_DIM) + EPS)
        return a * inv * g[:, :NOPE_DIM], b * inv * g[:, NOPE_DIM:]

    cqn = _rms(cq_ref[...], gqa_ref[...]).astype(BF16)
    qf = _dot(cqn, wuq_ref[...])
    gq = gq_ref[...]
    for h in range(N_HEADS_A):
        a = qf[:, h * QK_PAD:h * QK_PAD + NOPE_DIM]
        b = rope(qf[:, h * QK_PAD + NOPE_DIM:(h + 1) * QK_PAD])
        a, b = head_norm(a, b, gq)
        q_ref[0, h, :, 0:NOPE_DIM] = a.astype(q_ref.dtype)
        q_ref[0, h, :, NOPE_DIM:QK_PAD] = b.astype(q_ref.dtype)

    ckvn = _rms(ckv_ref[...], gkva_ref[...])
    lat_ref[0] = ckvn
    cb = ckvn.astype(BF16)
    kn = _dot(cb, wuk_ref[...])
    vf = _dot(cb, wuv_ref[...])
    krr = rope(kr_ref[...])
    kro_ref[0] = krr[:, :ROPE_DIM]
    gk = gk_ref[...]
    for h in range(N_HEADS_A):
        a, b = head_norm(kn[:, h * NOPE_DIM:(h + 1) * NOPE_DIM], krr, gk)
        k_ref[0, h, :, 0:NOPE_DIM] = a.astype(BF16)
        k_ref[0, h, :, NOPE_DIM:QK_PAD] = b.astype(BF16)
        v_ref[0, h] = vf[:, h * V_DIM:(h + 1) * V_DIM].astype(BF16)


def mla_prep(proj, nb, seq, lw, tabs, *, tm, q_dtype):
    nt = seq // tm
    row = lambda b, t: (b * nt + t, 0)
    const = lambda b, t: (0, 0)
    tab = lambda b, t: (t, 0)
    h = N_HEADS_A
    return pl.pallas_call(
        _mla_prep_kernel,
        out_shape=(jax.ShapeDtypeStruct((nb, h, seq, QK_PAD), q_dtype),
                   jax.ShapeDtypeStruct((nb, h, seq, QK_PAD), BF16),
                   jax.ShapeDtypeStruct((nb, h, seq, V_DIM), BF16),
                   jax.ShapeDtypeStruct((nb, seq, KV_LORA), F32),
                   jax.ShapeDtypeStruct((nb, seq, ROPE_DIM), F32)),
        grid=(nb, nt),
        in_specs=[pl.BlockSpec((tm, Q_LORA), lambda b, t: (b * nt + t, COL_CQ // Q_LORA)),
                  pl.BlockSpec((tm, KV_LORA), lambda b, t: (b * nt + t, COL_CKV // KV_LORA)),
                  pl.BlockSpec((tm, LANE), lambda b, t: (b * nt + t, COL_KR // LANE)),
                  pl.BlockSpec((1, Q_LORA), const),
                  pl.BlockSpec((1, KV_LORA), const),
                  pl.BlockSpec((Q_LORA, h * QK_PAD), const),
                  pl.BlockSpec((KV_LORA, h * NOPE_DIM), const),
                  pl.BlockSpec((KV_LORA, h * V_DIM), const),
                  pl.BlockSpec((1, QK_PAD), const),
                  pl.BlockSpec((1, QK_PAD), const),
                  pl.BlockSpec((tm, LANE), tab),
                  pl.BlockSpec((tm, LANE), tab),
                  pl.BlockSpec((tm, LANE), tab)],
        out_specs=(pl.BlockSpec((1, h, tm, QK_PAD), lambda b, t: (b, 0, t, 0)),
                   pl.BlockSpec((1, h, tm, QK_PAD), lambda b, t: (b, 0, t, 0)),
                   pl.BlockSpec((1, h, tm, V_DIM), lambda b, t: (b, 0, t, 0)),
                   pl.BlockSpec((1, tm, KV_LORA), lambda b, t: (b, t, 0)),
                   pl.BlockSpec((1, tm, ROPE_DIM), lambda b, t: (b, t, 0))),
        compiler_params=_cparams(("parallel", "parallel")),
        name="mla_prep",
    )(proj, proj, proj, lw['g_q_a'], lw['g_kv_a'], lw['w_uq'], lw['w_uk'], lw['w_uv'], lw['gq'], lw['gk'],
      tabs[0], tabs[1], tabs[2])


def _flash_kernel(qi_ref, ki_ref, q_ref, k_ref, v_ref, o_ref, m_ref, l_ref, acc_ref, *, tq, tk):
    p = pl.program_id(2)
    qi = qi_ref[p]
    ki = ki_ref[p]

    @pl.when(ki == 0)
    def _():
        m_ref[...] = jnp.full(m_ref.shape, -jnp.inf, F32)
        l_ref[...] = jnp.zeros(l_ref.shape, F32)
        acc_ref[...] = jnp.zeros(acc_ref.shape, F32)

    s = _dot_nt(q_ref[0, 0], k_ref[0, 0])
    row = qi * tq + lax.broadcasted_iota(jnp.int32, (tq, tk), 0)
    col = ki * tk + lax.broadcasted_iota(jnp.int32, (tq, tk), 1)
    s = jnp.where(col <= row, s, -jnp.inf)
    m_prev = m_ref[...]
    m_new = jnp.maximum(m_prev, jnp.max(s, axis=-1, keepdims=True))
    alpha = jnp.exp(m_prev - m_new)
    pr = jnp.exp(s - m_new)
    l_ref[...] = alpha * l_ref[...] + jnp.sum(pr, axis=-1, keepdims=True)
    acc_ref[...] = alpha * acc_ref[...] + _dot(pr.astype(BF16), v_ref[0, 0])
    m_ref[...] = m_new

    @pl.when((ki + 1) * tk >= (qi + 1) * tq)
    def _():
        o_ref[0] = (acc_ref[...] / l_ref[...]).astype(BF16)


def flash_attention(q, k, v, *, tq, tk):
    b, h, s, _ = q.shape
    pairs = [(i, j) for i in range(s // tq) for j in range(s // tk) if j * tk <= i * tq + tq - 1]
    qi_tab = jnp.asarray(np.array([p[0] for p in pairs], np.int32))
    ki_tab = jnp.asarray(np.array([p[1] for p in pairs], np.int32))
    return pl.pallas_call(
        functools.partial(_flash_kernel, tq=tq, tk=tk),
        out_shape=jax.ShapeDtypeStruct((b, s, h * V_DIM), BF16),
        grid_spec=pltpu.PrefetchScalarGridSpec(
            num_scalar_prefetch=2,
            grid=(b, h, len(pairs)),
            in_specs=[pl.BlockSpec((1, 1, tq, QK_PAD), lambda bi, hi, p, qt, kt: (bi, hi, qt[p], 0)),
                      pl.BlockSpec((1, 1, tk, QK_PAD), lambda bi, hi, p, qt, kt: (bi, hi, kt[p], 0)),
                      pl.BlockSpec((1, 1, tk, V_DIM), lambda bi, hi, p, qt, kt: (bi, hi, kt[p], 0))],
            out_specs=pl.BlockSpec((1, tq, V_DIM), lambda bi, hi, p, qt, kt: (bi, qt[p], hi)),
            scratch_shapes=[pltpu.VMEM((tq, 1), F32), pltpu.VMEM((tq, 1), F32), pltpu.VMEM((tq, V_DIM), F32)]),
        compiler_params=_cparams(("parallel", "parallel", "arbitrary")),
        name="flash_attention",
    )(qi_tab, ki_tab, q, k, v)


def _dec_q_kernel(q_ref, gk_ref, wuk_ref, qabs_ref, qgr_ref):
    gk = gk_ref[...]
    nb = q_ref.shape[2]
    for h in range(N_HEADS_A):
        qg = q_ref[0, h] * gk
        qn = qg[:, :NOPE_DIM].astype(BF16)
        qabs_ref[h] = _dot_nt(qn, wuk_ref[:, h * NOPE_DIM:(h + 1) * NOPE_DIM]).astype(BF16)
        qgr_ref[h] = qg[:, NOPE_DIM:QK_DIM].astype(BF16)
    for h in range(N_HEADS_A, 2 * N_HEADS_A):
        qabs_ref[h] = jnp.zeros((nb, KV_LORA), BF16)
        qgr_ref[h] = jnp.zeros((nb, ROPE_DIM), BF16)


def dec_q(q, gk, wuk):
    nb = q.shape[2]
    return pl.pallas_call(
        _dec_q_kernel,
        out_shape=(jax.ShapeDtypeStruct((2 * N_HEADS_A, nb, KV_LORA), BF16),
                   jax.ShapeDtypeStruct((2 * N_HEADS_A, nb, ROPE_DIM), BF16)),
        compiler_params=pltpu.CompilerParams(vmem_limit_bytes=VMEM_LIMIT),
        name="dec_q",
    )(q, gk, wuk)


def _dec_paged_kernel(pt_ref, *refs, pp):
    lat_refs = refs[:pp]
    kr_refs = refs[pp:2 * pp]
    wukt_ref, qabs_ref, qgr_ref, m_out, l_out, acc_out, lhs_ref, cbuf, m_s, l_s, acc_s = refs[2 * pp:]
    j = pl.program_id(1)
    nh = N_HEADS_A
    n_k = nh * NOPE_DIM

    @pl.when(j == 0)
    def _():
        lhs_ref[0:n_k, :] = wukt_ref[...]
        lhs_ref[n_k:n_k + 2 * nh, :] = qabs_ref[0]
        m_s[...] = jnp.full(m_s.shape, -jnp.inf, F32)
        l_s[...] = jnp.zeros(l_s.shape, F32)
        acc_s[...] = jnp.zeros(acc_s.shape, F32)

    for i in range(pp):
        cbuf[i * PAGE:(i + 1) * PAGE, :] = lat_refs[i][...].astype(BF16)

    ones = jnp.ones((2 * nh, ROPE_DIM), BF16)
    qgr = qgr_ref[0]
    for t in range(pp // 2):
        cb = cbuf[t * 2 * PAGE:(t + 1) * 2 * PAGE, :]
        kt = _dot_nt(lhs_ref[...], cb)
        k3 = kt[:n_k].reshape(nh, NOPE_DIM, 2 * PAGE)
        ssq = jnp.sum(k3 * k3, axis=1)
        num = kt[n_k:n_k + nh]
        kr = jnp.concatenate([kr_refs[2 * t][...], kr_refs[2 * t + 1][...]], axis=0)
        num = num + _dot_nt(qgr, kr.astype(BF16))[:nh]
        kr2 = kr * kr
        hi = kr2.astype(BF16)
        lo = (kr2 - hi.astype(F32)).astype(BF16)
        krsq = (_dot_nt(ones, hi) + _dot_nt(ones, lo))[:nh]
        s = num * lax.rsqrt((ssq + krsq) * (1.0 / QK_DIM) + EPS)
        m_prev = m_s[...]
        m_new = jnp.maximum(m_prev, jnp.max(s, axis=-1, keepdims=True))
        alpha = jnp.exp(m_prev - m_new)
        p = jnp.exp(s - m_new)
        l_s[...] = alpha * l_s[...] + jnp.sum(p, axis=-1, keepdims=True)
        pb = jnp.concatenate([p, jnp.zeros_like(p)], axis=0).astype(BF16)
        acc_s[...] = alpha * acc_s[...] + _dot(pb, cb)[:nh]
        m_s[...] = m_new

    @pl.when(j == pl.num_programs(1) - 1)
    def _():
        m_out[0] = jnp.broadcast_to(m_s[...], (nh, LANE))
        l_out[0] = jnp.broadcast_to(l_s[...], (nh, LANE))
        acc_out[0] = acc_s[...]


def dec_paged(li, page_table, cache_lat, cache_kr, wukt, qabs, qgr, *, pp):
    nb, n_pages = page_table.shape
    nh = N_HEADS_A
    lat_specs = [pl.BlockSpec((None, None, PAGE, KV_LORA), lambda b, j, pt, i=i: (li, pt[b, j * pp + i], 0, 0))
                 for i in range(pp)]
    kr_specs = [pl.BlockSpec((None, None, PAGE, ROPE_DIM), lambda b, j, pt, i=i: (li, pt[b, j * pp + i], 0, 0))
                for i in range(pp)]
    return pl.pallas_call(
        functools.partial(_dec_paged_kernel, pp=pp),
        out_shape=(jax.ShapeDtypeStruct((nb, nh, LANE), F32),
                   jax.ShapeDtypeStruct((nb, nh, LANE), F32),
                   jax.ShapeDtypeStruct((nb, nh, KV_LORA), F32)),
        grid_spec=pltpu.PrefetchScalarGridSpec(
            num_scalar_prefetch=1,
            grid=(nb, n_pages // pp),
            in_specs=lat_specs + kr_specs + [
                pl.BlockSpec((nh * NOPE_DIM, KV_LORA), lambda b, j, pt: (0, 0)),
                pl.BlockSpec((1, 2 * nh, KV_LORA), lambda b, j, pt: (b, 0, 0)),
                pl.BlockSpec((1, 2 * nh, ROPE_DIM), lambda b, j, pt: (b, 0, 0))],
            out_specs=(pl.BlockSpec((1, nh, LANE), lambda b, j, pt: (b, 0, 0)),
                       pl.BlockSpec((1, nh, LANE), lambda b, j, pt: (b, 0, 0)),
                       pl.BlockSpec((1, nh, KV_LORA), lambda b, j, pt: (b, 0, 0))),
            scratch_shapes=[pltpu.VMEM((nh * NOPE_DIM + 2 * nh, KV_LORA), BF16),
                            pltpu.VMEM((pp * PAGE, KV_LORA), BF16),
                            pltpu.VMEM((nh, 1), F32), pltpu.VMEM((nh, 1), F32), pltpu.VMEM((nh, KV_LORA), F32)]),
        compiler_params=_cparams(("parallel", "arbitrary")),
        name="dec_paged",
    )(page_table, *([cache_lat] * pp), *([cache_kr] * pp), wukt, qabs, qgr)


def _dec_finish_kernel(m_ref, l_ref, acc_ref, q_ref, k_ref, lat_ref, wuv_ref, o_ref):
    cn = lat_ref[...].astype(BF16).astype(F32)
    for h in range(N_HEADS_A):
        qh = q_ref[0, h].astype(BF16).astype(F32)
        s_new = jnp.sum(qh * k_ref[0, h].astype(F32), axis=-1, keepdims=True)
        m_h = m_ref[h][:, 0:1]
        l_h = l_ref[h][:, 0:1]
        m_tot = jnp.maximum(m_h, s_new)
        a = jnp.exp(m_h - m_tot)
        pn = jnp.exp(s_new - m_tot)
        lat = (acc_ref[h] * a + pn * cn) / (l_h * a + pn)
        o_ref[:, h * V_DIM:(h + 1) * V_DIM] = _dot(lat.astype(BF16), wuv_ref[:, h * V_DIM:(h + 1) * V_DIM]).astype(BF16)


def dec_finish(m, l, acc, q, k, lat, wuv):
    nb = lat.shape[0]
    return pl.pallas_call(
        _dec_finish_kernel,
        out_shape=jax.ShapeDtypeStruct((nb, N_HEADS_A * V_DIM), BF16),
        compiler_params=pltpu.CompilerParams(vmem_limit_bytes=VMEM_LIMIT),
        name="dec_finish",
    )(m, l, acc, q, k, lat, wuv)


def _ssd_prompt_kernel(xbc_ref, dt_ref, z_ref, cw_ref, cb_ref, dtb_ref, alog_ref, dsk_ref, g_ref, exp_ref,
                       y_ref, st_ref, xbuf, state):
    c = pl.program_id(1)
    lc = SSD_CHUNK
    hd = SSM_HEAD_DIM

    @pl.when(c == 0)
    def _():
        xbuf[0:8, :] = jnp.zeros((8, CONV_DIM), F32)
        state[...] = jnp.zeros(state.shape, F32)

    xbuf[8:8 + lc, :] = xbc_ref[...]
    cw = cw_ref[...]
    y = cb_ref[...]
    for i in range(SSM_CONV):
        y = y + cw[i:i + 1, :] * xbuf[8 - (SSM_CONV - 1) + i:8 - (SSM_CONV - 1) + i + lc, :]
    xbuf[0:8, :] = xbc_ref[lc - 8:lc, :]
    xc = _silu(y)
    x = xc[:, :D_SSM]
    dt = _softplus(dt_ref[...] + dtb_ref[...])
    da = dt * (-jnp.exp(alog_ref[...]))
    ii = lax.broadcasted_iota(jnp.int32, (lc, lc), 0)
    jj = lax.broadcasted_iota(jnp.int32, (lc, lc), 1)
    causal = jj <= ii
    acs = jnp.dot(causal.astype(F32), da, precision=HIGHEST, preferred_element_type=F32)
    acs_t = acs.T
    xdt = x * jnp.dot(dt, exp_ref[...], precision=HIGHEST, preferred_element_type=F32)
    xdt_t = xdt.T.astype(BF16)
    xdt_b = xdt.astype(BF16)
    ys = []
    for g in range(N_GROUPS_S):
        bm = xc[:, D_SSM + g * D_STATE:D_SSM + (g + 1) * D_STATE]
        cm = xc[:, D_SSM + (N_GROUPS_S + g) * D_STATE:D_SSM + (N_GROUPS_S + g + 1) * D_STATE]
        cmb = cm.astype(BF16)
        gmat = _dot_nt(cmb, bm.astype(BF16))
        for hh in range(N_HEADS_S // N_GROUPS_S):
            h = g * (N_HEADS_S // N_GROUPS_S) + hh
            a_col = acs[:, h:h + 1]
            a_row = acs_t[h:h + 1, :]
            a_last = acs[lc - 1:lc, h:h + 1]
            lmat = jnp.exp(jnp.where(causal, a_col - a_row, -jnp.inf))
            y_diag = _dot((gmat * lmat).astype(BF16), xdt_b[:, h * hd:(h + 1) * hd])
            st_prev = state[h]
            y_off = _dot_nt(cmb, st_prev.astype(BF16)) * jnp.exp(a_col)
            bdec = (bm * jnp.exp(a_last - a_col)).astype(BF16)
            state[h] = jnp.exp(a_last) * st_prev + _dot(xdt_t[h * hd:(h + 1) * hd, :], bdec)
            ys.append(y_diag + y_off)
    yy = jnp.concatenate(ys, axis=1) + x * dsk_ref[...]
    y_ref[...] = _rms(yy * _silu(z_ref[...]), g_ref[...]).astype(BF16)

    @pl.when(c == pl.num_programs(1) - 1)
    def _():
        st_ref[0] = state[...]


def ssd_prompt(proj, nb, seq, lw):
    nc = seq // SSD_CHUNK
    lc = SSD_CHUNK
    const = lambda b, c: (0, 0)
    return pl.pallas_call(
        _ssd_prompt_kernel,
        out_shape=(jax.ShapeDtypeStruct((nb * seq, D_SSM), BF16),
                   jax.ShapeDtypeStruct((nb, N_HEADS_S, SSM_HEAD_DIM, D_STATE), F32)),
        grid=(nb, nc),
        in_specs=[pl.BlockSpec((lc, CONV_DIM), lambda b, c: (b * nc + c, COL_XBC // CONV_DIM)),
                  pl.BlockSpec((lc, LANE), lambda b, c: (b * nc + c, COL_DT // LANE)),
                  pl.BlockSpec((lc, D_SSM), lambda b, c: (b * nc + c, COL_Z // D_SSM)),
                  pl.BlockSpec((SSM_CONV, CONV_DIM), const),
                  pl.BlockSpec((1, CONV_DIM), const),
                  pl.BlockSpec((1, LANE), const),
                  pl.BlockSpec((1, LANE), const),
                  pl.BlockSpec((1, D_SSM), const),
                  pl.BlockSpec((1, D_SSM), const),
                  pl.BlockSpec((LANE, D_SSM), const)],
        out_specs=(pl.BlockSpec((lc, D_SSM), lambda b, c: (b * nc + c, 0)),
                   pl.BlockSpec((1, N_HEADS_S, SSM_HEAD_DIM, D_STATE), lambda b, c: (b, 0, 0, 0))),
        scratch_shapes=[pltpu.VMEM((8 + lc, CONV_DIM), F32),
                        pltpu.VMEM((N_HEADS_S, SSM_HEAD_DIM, D_STATE), F32)],
        compiler_params=_cparams(("parallel", "arbitrary")),
        name="ssd_prompt",
    )(proj, proj, proj, lw['ssm_conv_w'], lw['ssm_conv_b'], lw['dt_bias'], lw['a_log'], lw['d_skip'],
      lw['g_ssm_out'], lw['head_expand'])


def _ssd_sample_prep_kernel(xbc_ref, p0_ref, p1_ref, p2_ref, dt_ref, cw_ref, cb_ref, dtb_ref, alog_ref, exp_ref,
                            x_ref, bc_ref, xdt_t_ref, dec_t_ref):
    cw = cw_ref[...]
    y = (cb_ref[...] + cw[0:1] * p0_ref[...] + cw[1:2] * p1_ref[...] + cw[2:3] * p2_ref[...]
         + cw[3:4] * xbc_ref[...])
    xc = _silu(y)
    x = xc[:, :D_SSM]
    dt = _softplus(dt_ref[...] + dtb_ref[...])
    dec = jnp.exp(dt * (-jnp.exp(alog_ref[...])))
    e = exp_ref[...]
    x_ref[...] = x
    bc_ref[...] = xc[:, D_SSM:]
    xdt_t_ref[...] = (x * jnp.dot(dt, e, precision=HIGHEST, preferred_element_type=F32)).T
    dec_t_ref[...] = jnp.dot(dec, e, precision=HIGHEST, preferred_element_type=F32).T


def ssd_sample_prep(proj, conv_state, lw):
    nb = proj.shape[0]
    const = lambda i: (0, 0)
    return pl.pallas_call(
        _ssd_sample_prep_kernel,
        out_shape=(jax.ShapeDtypeStruct((nb, D_SSM), F32),
                   jax.ShapeDtypeStruct((nb, 2 * N_GROUPS_S * D_STATE), F32),
                   jax.ShapeDtypeStruct((D_SSM, nb), F32),
                   jax.ShapeDtypeStruct((D_SSM, nb), F32)),
        grid=(1,),
        in_specs=[pl.BlockSpec((nb, CONV_DIM), lambda i: (0, COL_XBC // CONV_DIM)),
                  pl.BlockSpec((nb, CONV_DIM), lambda i: (0, 0)),
                  pl.BlockSpec((nb, CONV_DIM), lambda i: (0, 1)),
                  pl.BlockSpec((nb, CONV_DIM), lambda i: (0, 2)),
                  pl.BlockSpec((nb, LANE), lambda i: (0, COL_DT // LANE)),
                  pl.BlockSpec((SSM_CONV, CONV_DIM), const),
                  pl.BlockSpec((1, CONV_DIM), const),
                  pl.BlockSpec((1, LANE), const),
                  pl.BlockSpec((1, LANE), const),
                  pl.BlockSpec((LANE, D_SSM), const)],
        out_specs=(pl.BlockSpec((nb, D_SSM), const),
                   pl.BlockSpec((nb, 2 * N_GROUPS_S * D_STATE), const),
                   pl.BlockSpec((D_SSM, nb), const),
                   pl.BlockSpec((D_SSM, nb), const)),
        compiler_params=_cparams(("arbitrary",)),
        name="ssd_sample_prep",
    )(proj, conv_state, conv_state, conv_state, proj, lw['ssm_conv_w'], lw['ssm_conv_b'], lw['dt_bias'],
      lw['a_log'], lw['head_expand'])


def _ssd_sample_state_kernel(st_ref, xdt_t_ref, dec_t_ref, bc_ref, st_out, y_t_ref):
    b = pl.program_id(0)
    nb = xdt_t_ref.shape[1]
    lane = lax.broadcasted_iota(jnp.int32, (D_SSM, nb), 1)
    sel = lane == b
    xcol = jnp.sum(jnp.where(sel, xdt_t_ref[...], 0.0), axis=1, keepdims=True)
    dcol = jnp.sum(jnp.where(sel, dec_t_ref[...], 0.0), axis=1, keepdims=True)
    bc = bc_ref[pl.ds(b, 1), :]
    rows_per_group = D_SSM // N_GROUPS_S
    hs = st_ref[0].reshape(D_SSM, D_STATE)
    ycols = []
    for g in range(N_GROUPS_S):
        sl = slice(g * rows_per_group, (g + 1) * rows_per_group)
        brow = bc[:, g * D_STATE:(g + 1) * D_STATE]
        crow = bc[:, (N_GROUPS_S + g) * D_STATE:(N_GROUPS_S + g + 1) * D_STATE]
        new = hs[sl] * dcol[sl] + xcol[sl] * brow
        st_out[0, g * (N_HEADS_S // N_GROUPS_S):(g + 1) * (N_HEADS_S // N_GROUPS_S)] = new.reshape(
            N_HEADS_S // N_GROUPS_S, SSM_HEAD_DIM, D_STATE)
        ycols.append(jnp.sum(new * crow, axis=1, keepdims=True))
    ycol = jnp.concatenate(ycols, axis=0)

    @pl.when(b == 0)
    def _():
        y_t_ref[...] = jnp.zeros(y_t_ref.shape, F32)

    y_t_ref[...] = jnp.where(sel, ycol, y_t_ref[...])


def ssd_sample_state(state, xdt_t, dec_t, bc):
    nb = state.shape[0]
    const = lambda b: (0, 0)
    st_spec = pl.BlockSpec((1, N_HEADS_S, SSM_HEAD_DIM, D_STATE), lambda b: (b, 0, 0, 0))
    return pl.pallas_call(
        _ssd_sample_state_kernel,
        out_shape=(jax.ShapeDtypeStruct(state.shape, F32), jax.ShapeDtypeStruct((D_SSM, nb), F32)),
        grid=(nb,),
        in_specs=[st_spec, pl.BlockSpec((D_SSM, nb), const), pl.BlockSpec((D_SSM, nb), const),
                  pl.BlockSpec((nb, 2 * N_GROUPS_S * D_STATE), const)],
        out_specs=(st_spec, pl.BlockSpec((D_SSM, nb), const)),
        compiler_params=_cparams(("arbitrary",)),
        name="ssd_sample_state",
    )(state, xdt_t, dec_t, bc)


def _sample_mix_kernel(y_t_ref, x_ref, z_ref, u_ref, v_ref, dsk_ref, gs_ref, gv_ref, wv_ref, bv_ref,
                       ssm_ref, cm_ref, vn_ref):
    yy = y_t_ref[...].T + x_ref[...] * dsk_ref[...]
    ssm_ref[...] = _rms(yy * _silu(z_ref[...]), gs_ref[...]).astype(BF16)
    vn = _rms(v_ref[...], gv_ref[...])
    vn_ref[...] = vn
    cm_ref[...] = (u_ref[...] * (vn * wv_ref[...] + bv_ref[...])).astype(BF16)


def sample_mix(y_t, x, proj, lw):
    nb = x.shape[0]
    const = lambda i: (0, 0)
    blk = lambda col: pl.BlockSpec((nb, D_SSM), lambda i, col=col: (0, col // D_SSM))
    vec = pl.BlockSpec((1, D_SSM), const)
    return pl.pallas_call(
        _sample_mix_kernel,
        out_shape=(jax.ShapeDtypeStruct((nb, D_SSM), BF16), jax.ShapeDtypeStruct((nb, D_CM), BF16),
                   jax.ShapeDtypeStruct((nb, D_CM), F32)),
        grid=(1,),
        in_specs=[pl.BlockSpec((D_SSM, nb), const), pl.BlockSpec((nb, D_SSM), const),
                  blk(COL_Z), blk(COL_U), blk(COL_V), vec, vec, vec, vec, vec],
        out_specs=(pl.BlockSpec((nb, D_SSM), const), pl.BlockSpec((nb, D_CM), const),
                   pl.BlockSpec((nb, D_CM), const)),
        compiler_params=_cparams(("arbitrary",)),
        name="sample_mix",
    )(y_t, x, proj, proj, proj, lw['d_skip'], lw['g_ssm_out'], lw['g_sgu'], lw['sgu_w0'], lw['sgu_b0'])


def _spatial_gate_kernel(u_ref, v_ref, g_ref, w_ref, bt_ref, o_ref, *, n_chunks):
    vn = _rms(v_ref[...], g_ref[...]).astype(BF16)
    ii = lax.broadcasted_iota(jnp.int32, (CHUNK, CHUNK), 0)
    jj = lax.broadcasted_iota(jnp.int32, (CHUNK, CHUNK), 1)
    bt = bt_ref[...]
    for g in range(N_CG):
        w = jnp.where(jj <= ii, w_ref[g], 0.0).astype(BF16)
        bcol = bt[:, g:g + 1]
        for c in range(n_chunks):
            rs = slice(c * CHUNK, (c + 1) * CHUNK)
            cs = slice(g * CG_DIM, (g + 1) * CG_DIM)
            mixed = _dot(w, vn[rs, cs]) + bcol
            o_ref[rs, cs] = (u_ref[rs, cs] * mixed).astype(BF16)


def spatial_gate(proj, lw, *, tm):
    r = proj.shape[0]
    const = lambda i: (0, 0)
    return pl.pallas_call(
        functools.partial(_spatial_gate_kernel, n_chunks=tm // CHUNK),
        out_shape=jax.ShapeDtypeStruct((r, D_CM), BF16),
        grid=(r // tm,),
        in_specs=[pl.BlockSpec((tm, D_CM), lambda i: (i, COL_U // D_CM)),
                  pl.BlockSpec((tm, D_CM), lambda i: (i, COL_V // D_CM)),
                  pl.BlockSpec((1, D_CM), const),
                  pl.BlockSpec((N_CG, CHUNK, CHUNK), lambda i: (0, 0, 0)),
                  pl.BlockSpec((CHUNK, N_CG), const)],
        out_specs=pl.BlockSpec((tm, D_CM), lambda i: (i, 0)),
        compiler_params=_cparams(("parallel",)),
        name="spatial_gate",
    )(proj, proj, lw['g_sgu'], lw['w_spatial'], lw['b_spatial_t'])


def _out_proj_kernel(x_ref, att_ref, ssm_ref, cm_ref, w_ref, o_ref):
    na = N_HEADS_A * V_DIM
    acc = _dot(att_ref[...], w_ref[0:na, :])
    acc = acc + _dot(ssm_ref[...], w_ref[na:na + D_SSM, :])
    acc = acc + _dot(cm_ref[...], w_ref[na + D_SSM:, :])
    o_ref[...] = x_ref[...] + acc


def out_proj(x, att, ssm, cm, w, *, tm, tn):
    r, d = x.shape
    return pl.pallas_call(
        _out_proj_kernel,
        out_shape=jax.ShapeDtypeStruct((r, d), F32),
        grid=(r // tm, d // tn),
        in_specs=[pl.BlockSpec((tm, tn), lambda i, j: (i, j)),
                  pl.BlockSpec((tm, att.shape[1]), lambda i, j: (i, 0)),
                  pl.BlockSpec((tm, ssm.shape[1]), lambda i, j: (i, 0)),
                  pl.BlockSpec((tm, cm.shape[1]), lambda i, j: (i, 0)),
                  pl.BlockSpec((w.shape[0], tn), lambda i, j: (0, j))],
        out_specs=pl.BlockSpec((tm, tn), lambda i, j: (i, j)),
        compiler_params=_cparams(("parallel", "parallel")),
        name="out_proj",
    )(x, att, ssm, cm, w)


FFN_HALO = 16


def _ffn_prompt_kernel(x_ref, xp_ref, g_ref, wg_ref, wu_ref, cw_ref, cb_ref, wd_ref, o_ref, h_ref, gbuf, acc_ref,
                       *, tm, seq):
    i = pl.program_id(0)
    j = pl.program_id(1)

    @pl.when(j == 0)
    def _():
        keep = ((i * tm) % seq != 0).astype(F32)
        h_ref[0:FFN_HALO, :] = (_rms(xp_ref[...], g_ref[...]) * keep).astype(BF16)
        h_ref[FFN_HALO:, :] = _rms(x_ref[...], g_ref[...]).astype(BF16)
        acc_ref[...] = jnp.zeros(acc_ref.shape, F32)

    gbuf[...] = _dot(h_ref[...], wg_ref[...])
    up = _dot(h_ref[FFN_HALO:, :], wu_ref[...])
    cw = cw_ref[...]
    y = cb_ref[...]
    for t in range(FFN_CONV):
        off = FFN_HALO - (FFN_CONV - 1) + t
        y = y + cw[t:t + 1, :] * gbuf[off:off + tm, :]
    acc_ref[...] += _dot((_silu(y) * up).astype(BF16), wd_ref[...])

    @pl.when(j == pl.num_programs(1) - 1)
    def _():
        o_ref[...] = x_ref[...] + acc_ref[...]


def ffn_prompt(x, seq, lw, *, tm, tf):
    r, d = x.shape
    const = lambda i, j: (0, 0)
    halo_blocks = tm // FFN_HALO
    return pl.pallas_call(
        functools.partial(_ffn_prompt_kernel, tm=tm, seq=seq),
        out_shape=jax.ShapeDtypeStruct((r, d), F32),
        grid=(r // tm, D_FF // tf),
        in_specs=[pl.BlockSpec((tm, d), lambda i, j: (i, 0)),
                  pl.BlockSpec((FFN_HALO, d), lambda i, j: (jnp.maximum(i * halo_blocks - 1, 0), 0)),
                  pl.BlockSpec((1, d), const),
                  pl.BlockSpec((d, tf), lambda i, j: (0, j)),
                  pl.BlockSpec((d, tf), lambda i, j: (0, j)),
                  pl.BlockSpec((FFN_CONV, tf), lambda i, j: (0, j)),
                  pl.BlockSpec((1, tf), lambda i, j: (0, j)),
                  pl.BlockSpec((tf, d), lambda i, j: (j, 0))],
        out_specs=pl.BlockSpec((tm, d), lambda i, j: (i, 0)),
        scratch_shapes=[pltpu.VMEM((tm + FFN_HALO, d), BF16), pltpu.VMEM((tm + FFN_HALO, tf), F32),
                        pltpu.VMEM((tm, d), F32)],
        compiler_params=_cparams(("parallel", "arbitrary")),
        name="ffn_prompt",
    )(x, x, lw['g_ffn'], lw['w_gate'], lw['w_up'], lw['ffn_conv_w'], lw['ffn_conv_b'], lw['w_down'])


def _ffn_sample_kernel(x_ref, p0_ref, p1_ref, g_ref, wg_ref, wu_ref, cw_ref, cb_ref, wd_ref, o_ref, gate_ref,
                       h_ref, acc_ref):
    j = pl.program_id(0)

    @pl.when(j == 0)
    def _():
        h_ref[...] = _rms(x_ref[...], g_ref[...]).astype(BF16)
        acc_ref[...] = jnp.zeros(acc_ref.shape, F32)

    gate = _dot(h_ref[...], wg_ref[...])
    up = _dot(h_ref[...], wu_ref[...])
    gate_ref[...] = gate
    cw = cw_ref[...]
    y = cb_ref[...] + cw[0:1] * p0_ref[...] + cw[1:2] * p1_ref[...] + cw[2:3] * gate
    acc_ref[...] += _dot((_silu(y) * up).astype(BF16), wd_ref[...])

    @pl.when(j == pl.num_programs(0) - 1)
    def _():
        o_ref[...] = x_ref[...] + acc_ref[...]


def ffn_sample(x, prev, lw, *, tf):
    nb, d = x.shape
    nf = D_FF // tf
    return pl.pallas_call(
        _ffn_sample_kernel,
        out_shape=(jax.ShapeDtypeStruct((nb, d), F32), jax.ShapeDtypeStruct((nb, D_FF), F32)),
        grid=(nf,),
        in_specs=[pl.BlockSpec((nb, d), lambda j: (0, 0)),
                  pl.BlockSpec((nb, tf), lambda j: (0, j)),
                  pl.BlockSpec((nb, tf), lambda j: (0, nf + j)),
                  pl.BlockSpec((1, d), lambda j: (0, 0)),
                  pl.BlockSpec((d, tf), lambda j: (0, j)),
                  pl.BlockSpec((d, tf), lambda j: (0, j)),
                  pl.BlockSpec((FFN_CONV, tf), lambda j: (0, j)),
                  pl.BlockSpec((1, tf), lambda j: (0, j)),
                  pl.BlockSpec((tf, d), lambda j: (j, 0))],
        out_specs=(pl.BlockSpec((nb, d), lambda j: (0, 0)), pl.BlockSpec((nb, tf), lambda j: (0, j))),
        scratch_shapes=[pltpu.VMEM((nb, d), BF16), pltpu.VMEM((nb, d), F32)],
        compiler_params=_cparams(("arbitrary",)),
        name="ffn_sample",
    )(x, prev, prev, lw['g_ffn'], lw['w_gate'], lw['w_up'], lw['ffn_conv_w'], lw['ffn_conv_b'], lw['w_down'])


def _rope_tables(pos):
    inv = ROPE_THETA ** (-jnp.arange(0, ROPE_DIM, 2, dtype=F32) / ROPE_DIM)
    ang = pos.astype(F32)[:, None] * inv[None, :]
    cos, sin = jnp.cos(ang), jnp.sin(ang)
    zero = jnp.zeros_like(cos)
    return (jnp.concatenate([cos, cos, zero, zero], axis=1),
            jnp.concatenate([-sin, zero, zero, zero], axis=1),
            jnp.concatenate([zero, sin, zero, zero], axis=1))


def _pack_layer(li, p):
    d = D_MODEL
    w_in = p['w_in'][li]
    offs = np.cumsum([0, Q_LORA, KV_LORA, ROPE_DIM, D_SSM, CONV_DIM, N_HEADS_S, D_CM, D_CM])
    seg = lambda k: w_in[:, int(offs[k]):int(offs[k + 1])]
    zpad = lambda n: jnp.zeros((d, n), w_in.dtype)
    w_in_p = jnp.concatenate([seg(0), seg(3), seg(6), seg(7), seg(4), seg(1),
                              seg(2), zpad(LANE - ROPE_DIM), seg(5), zpad(LANE - N_HEADS_S)], axis=1).astype(BF16)
    h = N_HEADS_A
    w_uq = jnp.pad(p['w_uq'][li], ((0, 0), (0, 0), (0, QK_PAD - QK_DIM))).reshape(Q_LORA, h * QK_PAD).astype(BF16)
    w_uk = p['w_uk'][li].reshape(KV_LORA, h * NOPE_DIM).astype(BF16)
    w_uv = p['w_uv'][li].reshape(KV_LORA, h * V_DIM).astype(BF16)
    pad_g = lambda g: jnp.pad(g, (0, QK_PAD - QK_DIM))[None, :]
    row = lambda v: v[None, :]
    lane_pad = lambda v: jnp.pad(v, (0, LANE - v.shape[0]))[None, :]
    head_expand = np.zeros((LANE, D_SSM), np.float32)
    head_expand[:N_HEADS_S] = np.kron(np.eye(N_HEADS_S, dtype=np.float32), np.ones((1, SSM_HEAD_DIM), np.float32))
    head_expand = jnp.asarray(head_expand)
    w_gu = p['w_gate_up'][li]
    return {
        'g_mix': row(p['g_mix'][li]), 'w_in': w_in_p,
        'g_q_a': row(p['g_q_a'][li]), 'g_kv_a': row(p['g_kv_a'][li]),
        'w_uq': w_uq, 'w_uk': w_uk, 'w_uk_t': w_uk.T, 'w_uv': w_uv,
        'gq': pad_g(p['g_qk_q'][li]) * (QK_DIM ** -0.5),
        'gk': pad_g(p['g_qk_k'][li]),
        'ssm_conv_w': p['ssm_conv_w'][li], 'ssm_conv_b': row(p['ssm_conv_b'][li]),
        'dt_bias': lane_pad(p['ssm_dt_bias'][li]), 'a_log': lane_pad(p['ssm_a_log'][li]),
        'd_skip': row(jnp.repeat(p['ssm_d'][li], SSM_HEAD_DIM)),
        'g_ssm_out': row(p['g_ssm_out'][li]), 'head_expand': head_expand,
        'g_sgu': row(p['g_sgu'][li]), 'w_spatial': p['w_spatial'][li], 'b_spatial_t': p['b_spatial'][li].T,
        'sgu_w0': row(jnp.repeat(p['w_spatial'][li][:, 0, 0], CG_DIM)),
        'sgu_b0': row(jnp.repeat(p['b_spatial'][li][:, 0], CG_DIM)),
        'w_out': p['w_out'][li].astype(BF16),
        'g_ffn': row(p['g_ffn'][li]),
        'w_gate': w_gu[:, :D_FF].astype(BF16), 'w_up': w_gu[:, D_FF:].astype(BF16),
        'ffn_conv_w': p['ffn_conv_w'][li], 'ffn_conv_b': row(p['ffn_conv_b'][li]),
        'w_down': p['w_down'][li].astype(BF16),
    }


def _tile(n, pref):
    t = min(n, pref)
    assert n % t == 0, (n, pref)
    return t


def kernel(x_prompt, x_sample, cache_latent, cache_krope, state_ssm, state_conv, state_ffn_conv, page_table, g_mix, w_in, g_q_a, w_uq, g_kv_a, w_uk, w_uv, g_qk_q, g_qk_k, ssm_conv_w, ssm_conv_b, ssm_dt_bias, ssm_a_log, ssm_d, g_ssm_out, g_sgu, w_spatial, b_spatial, w_out, g_ffn, w_gate_up, ffn_conv_w, ffn_conv_b, w_down):
    params = dict(g_mix=g_mix, w_in=w_in, g_q_a=g_q_a, w_uq=w_uq, g_kv_a=g_kv_a, w_uk=w_uk, w_uv=w_uv,
                  g_qk_q=g_qk_q, g_qk_k=g_qk_k, ssm_conv_w=ssm_conv_w, ssm_conv_b=ssm_conv_b,
                  ssm_dt_bias=ssm_dt_bias, ssm_a_log=ssm_a_log, ssm_d=ssm_d, g_ssm_out=g_ssm_out, g_sgu=g_sgu,
                  w_spatial=w_spatial, b_spatial=b_spatial, w_out=w_out, g_ffn=g_ffn, w_gate_up=w_gate_up,
                  ffn_conv_w=ffn_conv_w, ffn_conv_b=ffn_conv_b, w_down=w_down)
    depth = w_in.shape[0]
    bp, seq, d = x_prompt.shape
    bd, dec_seq, _ = x_sample.shape
    assert dec_seq == 1 and d == D_MODEL and cache_latent.shape[2] == PAGE
    n_pages = page_table.shape[1]
    past_len = n_pages * PAGE
    rp = bp * seq

    tabs_p = _rope_tables(jnp.arange(seq, dtype=jnp.int32))
    tabs_s = _rope_tables(jnp.full((bd,), past_len, jnp.int32))

    tm_p = _tile(seq, 512)
    tq = _tile(seq, 1024)
    tk = _tile(seq, 512)
    pp = 8 if n_pages % 8 == 0 else 2
    tf = 512

    yp = x_prompt.reshape(rp, d)
    ys = x_sample.reshape(bd, d)
    outs = [[] for _ in range(11)]
    for li in range(depth):
        lw = _pack_layer(li, params)

        proj = norm_matmul(yp, lw['g_mix'], lw['w_in'], tm=tm_p, tn=512)
        q, k, v, lat, kro = mla_prep(proj, bp, seq, lw, tabs_p, tm=_tile(seq, 256), q_dtype=BF16)
        att = flash_attention(q, k, v, tq=tq, tk=tk)
        ssm, ssm_state = ssd_prompt(proj, bp, seq, lw)
        cm = spatial_gate(proj, lw, tm=tm_p)
        y_mid = out_proj(yp, att.reshape(rp, -1), ssm, cm, lw['w_out'], tm=tm_p, tn=512)
        yp = ffn_prompt(y_mid, seq, lw, tm=tm_p, tf=tf)
        tail = y_mid.reshape(bp, seq, d)[:, seq - (FFN_CONV - 1):, :].reshape(bp * (FFN_CONV - 1), d)
        tail = jnp.pad(tail, ((0, 8 - tail.shape[0]), (0, 0)))
        gate_tail = norm_matmul(tail, lw['g_ffn'], lw['w_gate'], tm=8, tn=tf)[:bp * (FFN_CONV - 1)]
        proj3 = proj.reshape(bp, seq, IN_PAD)
        outs[0].append(lat)
        outs[1].append(kro)
        outs[2].append(ssm_state)
        outs[3].append(proj3[:, seq - (SSM_CONV - 1):, COL_XBC:COL_XBC + CONV_DIM])
        outs[4].append(gate_tail.reshape(bp, FFN_CONV - 1, D_FF))

        proj_s = norm_matmul(ys, lw['g_mix'], lw['w_in'], tm=bd, tn=512)
        q_s, k_s, _, lat_s, kro_s = mla_prep(proj_s, 1, bd, lw, tabs_s, tm=bd, q_dtype=F32)
        qabs, qgr = dec_q(q_s, lw['gk'], lw['w_uk'])
        m, l, acc = dec_paged(li, page_table, cache_latent, cache_krope, lw['w_uk_t'],
                              qabs.transpose(1, 0, 2), qgr.transpose(1, 0, 2), pp=pp)
        att_s = dec_finish(m.transpose(1, 0, 2), l.transpose(1, 0, 2), acc.transpose(1, 0, 2), q_s, k_s,
                           lat_s[0], lw['w_uv'])
        conv_prev = state_conv[li]
        x_s, bc_s, xdt_t, dec_t = ssd_sample_prep(proj_s, conv_prev.reshape(bd, -1), lw)
        ssm_state_s, y_t = ssd_sample_state(state_ssm[li], xdt_t, dec_t, bc_s)
        ssm_s, cm_s, vn_s = sample_mix(y_t, x_s, proj_s, lw)
        ys_mid = out_proj(ys, att_s, ssm_s, cm_s, lw['w_out'], tm=bd, tn=512)
        ffn_prev = state_ffn_conv[li]
        ys, gate_s = ffn_sample(ys_mid, ffn_prev.reshape(bd, -1), lw, tf=tf)
        outs[5].append(lat_s.reshape(bd, 1, KV_LORA))
        outs[6].append(kro_s.reshape(bd, 1, ROPE_DIM))
        outs[7].append(ssm_state_s)
        outs[8].append(jnp.concatenate([conv_prev[:, 1:], proj_s[:, None, COL_XBC:COL_XBC + CONV_DIM]], axis=1))
        outs[9].append(jnp.concatenate([ffn_prev[:, 1:], gate_s[:, None, :]], axis=1))
        outs[10].append(vn_s.reshape(bd, 1, D_CM))

    stacked = [jnp.stack(o) for o in outs]
    return (yp.reshape(bp, seq, d), ys.reshape(bd, 1, d), *stacked)
```

```python
import functools

import numpy as np
import jax
import jax.numpy as jnp
from jax import lax
from jax.experimental import pallas as pl
from jax.experimental.pallas import tpu as pltpu

F32 = jnp.float32
BF16 = jnp.bfloat16
EPS = 1e-6
HIGHEST = lax.Precision.HIGHEST

D_MODEL = 2048
N_HEADS_A = 8
NOPE_DIM = 128
ROPE_DIM = 64
QK_DIM = NOPE_DIM + ROPE_DIM
QK_PAD = 256
V_DIM = 128
Q_LORA = 512
KV_LORA = 256
ROPE_THETA = 10000.0
D_SSM = 512
SSM_HEAD_DIM = 64
N_HEADS_S = 8
N_GROUPS_S = 2
D_STATE = 128
SSM_CONV = 4
SSD_CHUNK = 128
CONV_DIM = 1024
D_CM = 512
CHUNK = 128
N_CG = 4
CG_DIM = 128
D_FF = 5632
FFN_CONV = 3
PAGE = 128

COL_CQ, COL_Z, COL_U, COL_V, COL_XBC, COL_CKV, COL_KR, COL_DT = 0, 512, 1024, 1536, 2048, 3072, 3328, 3456
IN_PAD = 3584

LANE = 128
VMEM_LIMIT = 56 * 1024 * 1024

NT_DIMS = (((1,), (1,)), ((), ()))


def _cparams(sem):
    return pltpu.CompilerParams(dimension_semantics=sem, vmem_limit_bytes=VMEM_LIMIT)


def _rms(x, g):
    return x * lax.rsqrt(jnp.mean(x * x, axis=-1, keepdims=True) + EPS) * g


def _silu(x):
    return x * (1.0 / (1.0 + jnp.exp(-x)))


def _softplus(x):
    return jnp.maximum(x, 0.0) + jnp.log(1.0 + jnp.exp(-jnp.abs(x)))


def _dot(a, b):
    return jnp.dot(a, b, preferred_element_type=F32)


def _dot_nt(a, b):
    return lax.dot_general(a, b, NT_DIMS, preferred_element_type=F32)


def _norm_matmul_kernel(x_ref, g_ref, w_ref, o_ref, h_ref):
    @pl.when(pl.program_id(1) == 0)
    def _():
        h_ref[...] = _rms(x_ref[...], g_ref[...]).astype(BF16)

    o_ref[...] = _dot(h_ref[...], w_ref[...])


def norm_matmul(x, g, w, *, tm, tn):
    r, k = x.shape
    n = w.shape[1]
    return pl.pallas_call(
        _norm_matmul_kernel,
        out_shape=jax.ShapeDtypeStruct((r, n), F32),
        grid=(r // tm, n // tn),
        in_specs=[pl.BlockSpec((tm, k), lambda i, j: (i, 0)),
                  pl.BlockSpec((1, k), lambda i, j: (0, 0)),
                  pl.BlockSpec((k, tn), lambda i, j: (0, j))],
        out_specs=pl.BlockSpec((tm, tn), lambda i, j: (i, j)),
        scratch_shapes=[pltpu.VMEM((tm, k), BF16)],
        compiler_params=_cparams(("parallel", "arbitrary")),
        name="norm_matmul",
    )(x, g, w)


def _mla_prep_kernel(cq_ref, ckv_ref, kr_ref, gqa_ref, gkva_ref, wuq_ref, wuk_ref, wuv_ref, gq_ref, gk_ref,
                     cos_ref, s1_ref, s2_ref, q_ref, k_ref, v_ref, lat_ref, kro_ref):
    cos = cos_ref[...]
    s1 = s1_ref[...]
    s2 = s2_ref[...]

    def rope(b):
        return b * cos + pltpu.roll(b, 96, 1) * s1 + pltpu.roll(b, 32, 1) * s2

    def head_norm(a, b, g):
        ss = jnp.sum(a * a + b * b, axis=-1, keepdims=True)
        inv = lax.rsqrt(ss * (1.0 / QK_DIM) + EPS)
        return a * inv * g[:, :NOPE_DIM], b * inv * g[:, NOPE_DIM:]

    cqn = _rms(cq_ref[...], gqa_ref[...]).astype(BF16)
    qf = _dot(cqn, wuq_ref[...])
    gq = gq_ref[...]
    for h in range(N_HEADS_A):
        a = qf[:, h * QK_PAD:h * QK_PAD + NOPE_DIM]
        b = rope(qf[:, h * QK_PAD + NOPE_DIM:(h + 1) * QK_PAD])
        a, b = head_norm(a, b, gq)
        q_ref[0, h, :, 0:NOPE_DIM] = a.astype(q_ref.dtype)
        q_ref[0, h, :, NOPE_DIM:QK_PAD] = b.astype(q_ref.dtype)

    ckvn = _rms(ckv_ref[...], gkva_ref[...])
    lat_ref[0] = ckvn
    cb = ckvn.astype(BF16)
    kn = _dot(cb, wuk_ref[...])
    vf = _dot(cb, wuv_ref[...])
    krr = rope(kr_ref[...])
    kro_ref[0] = krr[:, :ROPE_DIM]
    gk = gk_ref[...]
    for h in range(N_HEADS_A):
        a, b = head_norm(kn[:, h * NOPE_DIM:(h + 1) * NOPE_DIM], krr, gk)
        k_ref[0, h, :, 0:NOPE_DIM] = a.astype(BF16)
        k_ref[0, h, :, NOPE_DIM:QK_PAD] = b.astype(BF16)
        v_ref[0, h] = vf[:, h * V_DIM:(h + 1) * V_DIM].astype(BF16)


def mla_prep(proj, nb, seq, lw, gq, tabs, *, tm, q_dtype):
    nt = seq // tm
    row = lambda b, t: (b * nt + t, 0)
    const = lambda b, t: (0, 0)
    tab = lambda b, t: (t, 0)
    h = N_HEADS_A
    return pl.pallas_call(
        _mla_prep_kernel,
        out_shape=(jax.ShapeDtypeStruct((nb, h, seq, QK_PAD), q_dtype),
                   jax.ShapeDtypeStruct((nb, h, seq, QK_PAD), BF16),
                   jax.ShapeDtypeStruct((nb, h, seq, V_DIM), BF16),
                   jax.ShapeDtypeStruct((nb, seq, KV_LORA), F32),
                   jax.ShapeDtypeStruct((nb, seq, ROPE_DIM), F32)),
        grid=(nb, nt),
        in_specs=[pl.BlockSpec((tm, Q_LORA), lambda b, t: (b * nt + t, COL_CQ // Q_LORA)),
                  pl.BlockSpec((tm, KV_LORA), lambda b, t: (b * nt + t, COL_CKV // KV_LORA)),
                  pl.BlockSpec((tm, LANE), lambda b, t: (b * nt + t, COL_KR // LANE)),
                  pl.BlockSpec((1, Q_LORA), const),
                  pl.BlockSpec((1, KV_LORA), const),
                  pl.BlockSpec((Q_LORA, h * QK_PAD), const),
                  pl.BlockSpec((KV_LORA, h * NOPE_DIM), const),
                  pl.BlockSpec((KV_LORA, h * V_DIM), const),
                  pl.BlockSpec((1, QK_PAD), const),
                  pl.BlockSpec((1, QK_PAD), const),
                  pl.BlockSpec((tm, LANE), tab),
                  pl.BlockSpec((tm, LANE), tab),
                  pl.BlockSpec((tm, LANE), tab)],
        out_specs=(pl.BlockSpec((1, h, tm, QK_PAD), lambda b, t: (b, 0, t, 0)),
                   pl.BlockSpec((1, h, tm, QK_PAD), lambda b, t: (b, 0, t, 0)),
                   pl.BlockSpec((1, h, tm, V_DIM), lambda b, t: (b, 0, t, 0)),
                   pl.BlockSpec((1, tm, KV_LORA), lambda b, t: (b, t, 0)),
                   pl.BlockSpec((1, tm, ROPE_DIM), lambda b, t: (b, t, 0))),
        compiler_params=_cparams(("parallel", "parallel")),
        name="mla_prep",
    )(proj, proj, proj, lw['g_q_a'], lw['g_kv_a'], lw['w_uq'], lw['w_uk'], lw['w_uv'], gq, lw['gk'],
      tabs[0], tabs[1], tabs[2])


FLASH_ROWS = 256
LOG2E = 1.4426950408889634


def _flash_kernel(qi_ref, ki_ref, q_ref, k_ref, v_ref, o_ref, m_ref, l_ref, acc_ref, *, tq, tk):
    p = pl.program_id(2)
    qi = qi_ref[p]
    ki = ki_ref[p]

    @pl.when(ki == 0)
    def _():
        m_ref[...] = jnp.full(m_ref.shape, -jnp.inf, F32)
        l_ref[...] = jnp.zeros(l_ref.shape, F32)
        acc_ref[...] = jnp.zeros(acc_ref.shape, F32)

    def update(r, n_cols, masked):
        rows = slice(r * FLASH_ROWS, (r + 1) * FLASH_ROWS)
        s = _dot_nt(q_ref[0, 0, rows, :], k_ref[0, 0, 0:n_cols, :])
        if masked:
            row = r * FLASH_ROWS + lax.broadcasted_iota(jnp.int32, s.shape, 0)
            col = lax.broadcasted_iota(jnp.int32, s.shape, 1)
            s = jnp.where(col <= row, s, -jnp.inf)
        m_prev = m_ref[rows, :]
        m_new = jnp.maximum(m_prev, jnp.max(s, axis=-1, keepdims=True))
        alpha = jnp.exp2(m_prev - m_new)
        pr = jnp.exp2(s - m_new)
        l_ref[rows, :] = alpha * l_ref[rows, :] + jnp.sum(pr, axis=-1, keepdims=True)
        acc_ref[rows, :] = alpha * acc_ref[rows, :] + _dot(pr.astype(BF16), v_ref[0, 0, 0:n_cols, :])
        m_ref[rows, :] = m_new

    @pl.when(ki < qi)
    def _():
        for r in range(tq // FLASH_ROWS):
            update(r, tk, False)

    @pl.when(ki == qi)
    def _():
        for r in range(tq // FLASH_ROWS):
            update(r, (r + 1) * FLASH_ROWS, True)
        o_ref[0] = (acc_ref[...] / l_ref[...]).astype(BF16)


def flash_attention(q, k, v, *, tq, tk):
    b, h, s, _ = q.shape
    assert tq == tk and tq % FLASH_ROWS == 0
    pairs = [(i, j) for i in range(s // tq) for j in range(i + 1)]
    qi_tab = jnp.asarray(np.array([p[0] for p in pairs], np.int32))
    ki_tab = jnp.asarray(np.array([p[1] for p in pairs], np.int32))
    return pl.pallas_call(
        functools.partial(_flash_kernel, tq=tq, tk=tk),
        out_shape=jax.ShapeDtypeStruct((b, s, h * V_DIM), BF16),
        grid_spec=pltpu.PrefetchScalarGridSpec(
            num_scalar_prefetch=2,
            grid=(b, h, len(pairs)),
            in_specs=[pl.BlockSpec((1, 1, tq, QK_PAD), lambda bi, hi, p, qt, kt: (bi, hi, qt[p], 0)),
                      pl.BlockSpec((1, 1, tk, QK_PAD), lambda bi, hi, p, qt, kt: (bi, hi, kt[p], 0)),
                      pl.BlockSpec((1, 1, tk, V_DIM), lambda bi, hi, p, qt, kt: (bi, hi, kt[p], 0))],
            out_specs=pl.BlockSpec((1, tq, V_DIM), lambda bi, hi, p, qt, kt: (bi, qt[p], hi)),
            scratch_shapes=[pltpu.VMEM((tq, 1), F32), pltpu.VMEM((tq, 1), F32), pltpu.VMEM((tq, V_DIM), F32)]),
        compiler_params=_cparams(("parallel", "parallel", "arbitrary")),
        name="flash_attention",
    )(qi_tab, ki_tab, q, k, v)


def _dec_q_kernel(q_ref, gk_ref, wuk_ref, qabs_ref, qgr_ref):
    gk = gk_ref[...]
    nb = q_ref.shape[2]
    for h in range(N_HEADS_A):
        qg = q_ref[0, h] * gk
        qn = qg[:, :NOPE_DIM].astype(BF16)
        qabs_ref[h] = _dot_nt(qn, wuk_ref[:, h * NOPE_DIM:(h + 1) * NOPE_DIM]).astype(BF16)
        qgr_ref[h] = qg[:, NOPE_DIM:QK_DIM].astype(BF16)
    for h in range(N_HEADS_A, 2 * N_HEADS_A):
        qabs_ref[h] = jnp.zeros((nb, KV_LORA), BF16)
        qgr_ref[h] = jnp.zeros((nb, ROPE_DIM), BF16)


def dec_q(q, gk, wuk):
    nb = q.shape[2]
    return pl.pallas_call(
        _dec_q_kernel,
        out_shape=(jax.ShapeDtypeStruct((2 * N_HEADS_A, nb, KV_LORA), BF16),
                   jax.ShapeDtypeStruct((2 * N_HEADS_A, nb, ROPE_DIM), BF16)),
        compiler_params=pltpu.CompilerParams(vmem_limit_bytes=VMEM_LIMIT),
        name="dec_q",
    )(q, gk, wuk)


def _dec_paged_kernel(pt_ref, *refs, pp):
    lat_refs = refs[:pp]
    kr_refs = refs[pp:2 * pp]
    wukt_ref, qabs_ref, qgr_ref, m_out, l_out, acc_out, lhs_ref, cbuf, m_s, l_s, acc_s = refs[2 * pp:]
    j = pl.program_id(1)
    nh = N_HEADS_A
    n_k = nh * NOPE_DIM

    @pl.when(j == 0)
    def _():
        lhs_ref[0:n_k, :] = wukt_ref[...]
        lhs_ref[n_k:n_k + 2 * nh, :] = qabs_ref[0]
        m_s[...] = jnp.full(m_s.shape, -jnp.inf, F32)
        l_s[...] = jnp.zeros(l_s.shape, F32)
        acc_s[...] = jnp.zeros(acc_s.shape, F32)

    for i in range(pp):
        cbuf[i * PAGE:(i + 1) * PAGE, :] = lat_refs[i][...].astype(BF16)

    qgr = qgr_ref[0]
    parts = []
    for t in range(pp // 2):
        cb = cbuf[t * 2 * PAGE:(t + 1) * 2 * PAGE, :]
        kt = _dot_nt(lhs_ref[...], cb)
        k3 = kt[:n_k].reshape(nh, NOPE_DIM, 2 * PAGE)
        ssq = jnp.sum(k3 * k3, axis=1)
        krt = jnp.concatenate([kr_refs[2 * t][...], kr_refs[2 * t + 1][...]], axis=1)
        ssq = ssq + jnp.sum(krt * krt, axis=0, keepdims=True)
        num = kt[n_k:n_k + nh] + _dot(qgr, krt.astype(BF16))[:nh]
        parts.append(num * lax.rsqrt(ssq * (1.0 / QK_DIM) + EPS))
    s = jnp.concatenate(parts, axis=1)
    m_prev = m_s[...]
    m_new = jnp.maximum(m_prev, jnp.max(s, axis=-1, keepdims=True))
    alpha = jnp.exp(m_prev - m_new)
    p = jnp.exp(s - m_new)
    l_s[...] = alpha * l_s[...] + jnp.sum(p, axis=-1, keepdims=True)
    pb = jnp.concatenate([p, jnp.zeros_like(p)], axis=0).astype(BF16)
    acc_s[...] = alpha * acc_s[...] + _dot(pb, cbuf[...])[:nh]
    m_s[...] = m_new

    @pl.when(j == pl.num_programs(1) - 1)
    def _():
        m_out[0] = jnp.broadcast_to(m_s[...], (nh, LANE))
        l_out[0] = jnp.broadcast_to(l_s[...], (nh, LANE))
        acc_out[0] = acc_s[...]


def dec_paged(li, page_table, cache_lat, cache_kr_t, wukt, qabs, qgr, *, pp):
    nb, n_pages = page_table.shape
    nh = N_HEADS_A
    lat_specs = [pl.BlockSpec((None, None, PAGE, KV_LORA), lambda b, j, pt, i=i: (li, pt[b, j * pp + i], 0, 0))
                 for i in range(pp)]
    kr_specs = [pl.BlockSpec((None, None, ROPE_DIM, PAGE), lambda b, j, pt, i=i: (li, pt[b, j * pp + i], 0, 0))
                for i in range(pp)]
    return pl.pallas_call(
        functools.partial(_dec_paged_kernel, pp=pp),
        out_shape=(jax.ShapeDtypeStruct((nb, nh, LANE), F32),
                   jax.ShapeDtypeStruct((nb, nh, LANE), F32),
                   jax.ShapeDtypeStruct((nb, nh, KV_LORA), F32)),
        grid_spec=pltpu.PrefetchScalarGridSpec(
            num_scalar_prefetch=1,
            grid=(nb, n_pages // pp),
            in_specs=lat_specs + kr_specs + [
                pl.BlockSpec((nh * NOPE_DIM, KV_LORA), lambda b, j, pt: (0, 0)),
                pl.BlockSpec((1, 2 * nh, KV_LORA), lambda b, j, pt: (b, 0, 0)),
                pl.BlockSpec((1, 2 * nh, ROPE_DIM), lambda b, j, pt: (b, 0, 0))],
            out_specs=(pl.BlockSpec((1, nh, LANE), lambda b, j, pt: (b, 0, 0)),
                       pl.BlockSpec((1, nh, LANE), lambda b, j, pt: (b, 0, 0)),
                       pl.BlockSpec((1, nh, KV_LORA), lambda b, j, pt: (b, 0, 0))),
            scratch_shapes=[pltpu.VMEM((nh * NOPE_DIM + 2 * nh, KV_LORA), BF16),
                            pltpu.VMEM((pp * PAGE, KV_LORA), BF16),
                            pltpu.VMEM((nh, 1), F32), pltpu.VMEM((nh, 1), F32), pltpu.VMEM((nh, KV_LORA), F32)]),
        compiler_params=_cparams(("parallel", "arbitrary")),
        name="dec_paged",
    )(page_table, *([cache_lat] * pp), *([cache_kr_t] * pp), wukt, qabs, qgr)


def _dec_finish_kernel(m_ref, l_ref, acc_ref, q_ref, k_ref, lat_ref, wuv_ref, o_ref):
    cn = lat_ref[...].astype(BF16).astype(F32)
    for h in range(N_HEADS_A):
        qh = q_ref[0, h].astype(BF16).astype(F32)
        s_new = jnp.sum(qh * k_ref[0, h].astype(F32), axis=-1, keepdims=True)
        m_h = m_ref[h][:, 0:1]
        l_h = l_ref[h][:, 0:1]
        m_tot = jnp.maximum(m_h, s_new)
        a = jnp.exp(m_h - m_tot)
        pn = jnp.exp(s_new - m_tot)
        lat = (acc_ref[h] * a + pn * cn) / (l_h * a + pn)
        o_ref[:, h * V_DIM:(h + 1) * V_DIM] = _dot(lat.astype(BF16), wuv_ref[:, h * V_DIM:(h + 1) * V_DIM]).astype(BF16)


def dec_finish(m, l, acc, q, k, lat, wuv):
    nb = lat.shape[0]
    return pl.pallas_call(
        _dec_finish_kernel,
        out_shape=jax.ShapeDtypeStruct((nb, N_HEADS_A * V_DIM), BF16),
        compiler_params=pltpu.CompilerParams(vmem_limit_bytes=VMEM_LIMIT),
        name="dec_finish",
    )(m, l, acc, q, k, lat, wuv)


def _ssd_prompt_kernel(xbc_ref, dt_ref, z_ref, cw_ref, cb_ref, dtb_ref, alog_ref, dsk_ref, g_ref, exp_ref,
                       y_ref, st_ref, xbuf, state):
    c = pl.program_id(1)
    lc = SSD_CHUNK
    hd = SSM_HEAD_DIM

    @pl.when(c == 0)
    def _():
        xbuf[0:8, :] = jnp.zeros((8, CONV_DIM), F32)
        state[...] = jnp.zeros(state.shape, F32)

    xbuf[8:8 + lc, :] = xbc_ref[...]
    cw = cw_ref[...]
    y = cb_ref[...]
    for i in range(SSM_CONV):
        y = y + cw[i:i + 1, :] * xbuf[8 - (SSM_CONV - 1) + i:8 - (SSM_CONV - 1) + i + lc, :]
    xbuf[0:8, :] = xbc_ref[lc - 8:lc, :]
    xc = _silu(y)
    x = xc[:, :D_SSM]
    dt = _softplus(dt_ref[...] + dtb_ref[...])
    da = dt * (-jnp.exp(alog_ref[...]))
    ii = lax.broadcasted_iota(jnp.int32, (lc, lc), 0)
    jj = lax.broadcasted_iota(jnp.int32, (lc, lc), 1)
    causal = jj <= ii
    acs = jnp.dot(causal.astype(F32), da, precision=HIGHEST, preferred_element_type=F32)
    acs_t = acs.T
    xdt = x * jnp.dot(dt, exp_ref[...], precision=HIGHEST, preferred_element_type=F32)
    xdt_t = xdt.T.astype(BF16)
    xdt_b = xdt.astype(BF16)
    ys = []
    for g in range(N_GROUPS_S):
        bm = xc[:, D_SSM + g * D_STATE:D_SSM + (g + 1) * D_STATE]
        cm = xc[:, D_SSM + (N_GROUPS_S + g) * D_STATE:D_SSM + (N_GROUPS_S + g + 1) * D_STATE]
        cmb = cm.astype(BF16)
        gmat = _dot_nt(cmb, bm.astype(BF16))
        for hh in range(N_HEADS_S // N_GROUPS_S):
            h = g * (N_HEADS_S // N_GROUPS_S) + hh
            a_col = acs[:, h:h + 1]
            a_row = acs_t[h:h + 1, :]
            a_last = acs[lc - 1:lc, h:h + 1]
            lmat = jnp.exp(jnp.where(causal, a_col - a_row, -jnp.inf))
            y_diag = _dot((gmat * lmat).astype(BF16), xdt_b[:, h * hd:(h + 1) * hd])
            st_prev = state[h]
            y_off = _dot_nt(cmb, st_prev.astype(BF16)) * jnp.exp(a_col)
            bdec = (bm * jnp.exp(a_last - a_col)).astype(BF16)
            state[h] = jnp.exp(a_last) * st_prev + _dot(xdt_t[h * hd:(h + 1) * hd, :], bdec)
            ys.append(y_diag + y_off)
    yy = jnp.concatenate(ys, axis=1) + x * dsk_ref[...]
    y_ref[...] = _rms(yy * _silu(z_ref[...]), g_ref[...]).astype(BF16)

    @pl.when(c == pl.num_programs(1) - 1)
    def _():
        st_ref[0] = state[...]


def ssd_prompt(proj, nb, seq, lw):
    nc = seq // SSD_CHUNK
    lc = SSD_CHUNK
    const = lambda b, c: (0, 0)
    return pl.pallas_call(
        _ssd_prompt_kernel,
        out_shape=(jax.ShapeDtypeStruct((nb * seq, D_SSM), BF16),
                   jax.ShapeDtypeStruct((nb, N_HEADS_S, SSM_HEAD_DIM, D_STATE), F32)),
        grid=(nb, nc),
        in_specs=[pl.BlockSpec((lc, CONV_DIM), lambda b, c: (b * nc + c, COL_XBC // CONV_DIM)),
                  pl.BlockSpec((lc, LANE), lambda b, c: (b * nc + c, COL_DT // LANE)),
                  pl.BlockSpec((lc, D_SSM), lambda b, c: (b * nc + c, COL_Z // D_SSM)),
                  pl.BlockSpec((SSM_CONV, CONV_DIM), const),
                  pl.BlockSpec((1, CONV_DIM), const),
                  pl.BlockSpec((1, LANE), const),
                  pl.BlockSpec((1, LANE), const),
                  pl.BlockSpec((1, D_SSM), const),
                  pl.BlockSpec((1, D_SSM), const),
                  pl.BlockSpec((LANE, D_SSM), const)],
        out_specs=(pl.BlockSpec((lc, D_SSM), lambda b, c: (b * nc + c, 0)),
                   pl.BlockSpec((1, N_HEADS_S, SSM_HEAD_DIM, D_STATE), lambda b, c: (b, 0, 0, 0))),
        scratch_shapes=[pltpu.VMEM((8 + lc, CONV_DIM), F32),
                        pltpu.VMEM((N_HEADS_S, SSM_HEAD_DIM, D_STATE), F32)],
        compiler_params=_cparams(("parallel", "arbitrary")),
        name="ssd_prompt",
    )(proj, proj, proj, lw['ssm_conv_w'], lw['ssm_conv_b'], lw['dt_bias'], lw['a_log'], lw['d_skip'],
      lw['g_ssm_out'], lw['head_expand'])


def _ssd_sample_prep_kernel(xbc_ref, p0_ref, p1_ref, p2_ref, dt_ref, cw_ref, cb_ref, dtb_ref, alog_ref, exp_ref,
                            x_ref, bc_ref, xdt_t_ref, dec_t_ref):
    cw = cw_ref[...]
    y = (cb_ref[...] + cw[0:1] * p0_ref[...] + cw[1:2] * p1_ref[...] + cw[2:3] * p2_ref[...]
         + cw[3:4] * xbc_ref[...])
    xc = _silu(y)
    x = xc[:, :D_SSM]
    dt = _softplus(dt_ref[...] + dtb_ref[...])
    dec = jnp.exp(dt * (-jnp.exp(alog_ref[...])))
    e = exp_ref[...]
    x_ref[...] = x
    bc_ref[...] = xc[:, D_SSM:]
    xdt_t_ref[...] = (x * jnp.dot(dt, e, precision=HIGHEST, preferred_element_type=F32)).T
    dec_t_ref[...] = jnp.dot(dec, e, precision=HIGHEST, preferred_element_type=F32).T


def ssd_sample_prep(proj, conv_state, lw):
    nb = proj.shape[0]
    const = lambda i: (0, 0)
    return pl.pallas_call(
        _ssd_sample_prep_kernel,
        out_shape=(jax.ShapeDtypeStruct((nb, D_SSM), F32),
                   jax.ShapeDtypeStruct((nb, 2 * N_GROUPS_S * D_STATE), F32),
                   jax.ShapeDtypeStruct((D_SSM, nb), F32),
                   jax.ShapeDtypeStruct((D_SSM, nb), F32)),
        grid=(1,),
        in_specs=[pl.BlockSpec((nb, CONV_DIM), lambda i: (0, COL_XBC // CONV_DIM)),
                  pl.BlockSpec((nb, CONV_DIM), lambda i: (0, 0)),
                  pl.BlockSpec((nb, CONV_DIM), lambda i: (0, 1)),
                  pl.BlockSpec((nb, CONV_DIM), lambda i: (0, 2)),
                  pl.BlockSpec((nb, LANE), lambda i: (0, COL_DT // LANE)),
                  pl.BlockSpec((SSM_CONV, CONV_DIM), const),
                  pl.BlockSpec((1, CONV_DIM), const),
                  pl.BlockSpec((1, LANE), const),
                  pl.BlockSpec((1, LANE), const),
                  pl.BlockSpec((LANE, D_SSM), const)],
        out_specs=(pl.BlockSpec((nb, D_SSM), const),
                   pl.BlockSpec((nb, 2 * N_GROUPS_S * D_STATE), const),
                   pl.BlockSpec((D_SSM, nb), const),
                   pl.BlockSpec((D_SSM, nb), const)),
        compiler_params=_cparams(("arbitrary",)),
        name="ssd_sample_prep",
    )(proj, conv_state, conv_state, conv_state, proj, lw['ssm_conv_w'], lw['ssm_conv_b'], lw['dt_bias'],
      lw['a_log'], lw['head_expand'])


def _ssd_sample_state_kernel(st_ref, xdt_t_ref, dec_t_ref, bc_ref, st_out, y_t_ref):
    b = pl.program_id(0)
    nb = xdt_t_ref.shape[1]
    lane = lax.broadcasted_iota(jnp.int32, (D_SSM, nb), 1)
    sel = lane == b
    xcol = jnp.sum(jnp.where(sel, xdt_t_ref[...], 0.0), axis=1, keepdims=True)
    dcol = jnp.sum(jnp.where(sel, dec_t_ref[...], 0.0), axis=1, keepdims=True)
    bc = bc_ref[pl.ds(b, 1), :]
    rows_per_group = D_SSM // N_GROUPS_S
    hs = st_ref[0].reshape(D_SSM, D_STATE)
    ycols = []
    for g in range(N_GROUPS_S):
        sl = slice(g * rows_per_group, (g + 1) * rows_per_group)
        brow = bc[:, g * D_STATE:(g + 1) * D_STATE]
        crow = bc[:, (N_GROUPS_S + g) * D_STATE:(N_GROUPS_S + g + 1) * D_STATE]
        new = hs[sl] * dcol[sl] + xcol[sl] * brow
        st_out[0, g * (N_HEADS_S // N_GROUPS_S):(g + 1) * (N_HEADS_S // N_GROUPS_S)] = new.reshape(
            N_HEADS_S // N_GROUPS_S, SSM_HEAD_DIM, D_STATE)
        ycols.append(jnp.sum(new * crow, axis=1, keepdims=True))
    ycol = jnp.concatenate(ycols, axis=0)

    @pl.when(b == 0)
    def _():
        y_t_ref[...] = jnp.zeros(y_t_ref.shape, F32)

    y_t_ref[...] = jnp.where(sel, ycol, y_t_ref[...])


def ssd_sample_state(li, state, xdt_t, dec_t, bc):
    nb = state.shape[1]
    const = lambda b: (0, 0)
    st_spec = pl.BlockSpec((1, N_HEADS_S, SSM_HEAD_DIM, D_STATE), lambda b: (b, 0, 0, 0))
    st_in_spec = pl.BlockSpec((None, 1, N_HEADS_S, SSM_HEAD_DIM, D_STATE), lambda b: (li, b, 0, 0, 0))
    return pl.pallas_call(
        _ssd_sample_state_kernel,
        out_shape=(jax.ShapeDtypeStruct(state.shape[1:], F32), jax.ShapeDtypeStruct((D_SSM, nb), F32)),
        grid=(nb,),
        in_specs=[st_in_spec, pl.BlockSpec((D_SSM, nb), const), pl.BlockSpec((D_SSM, nb), const),
                  pl.BlockSpec((nb, 2 * N_GROUPS_S * D_STATE), const)],
        out_specs=(st_spec, pl.BlockSpec((D_SSM, nb), const)),
        compiler_params=_cparams(("arbitrary",)),
        name="ssd_sample_state",
    )(state, xdt_t, dec_t, bc)


def _sample_mix_kernel(y_t_ref, x_ref, z_ref, u_ref, v_ref, dsk_ref, gs_ref, gv_ref, wv_ref, bv_ref,
                       ssm_ref, cm_ref, vn_ref):
    yy = y_t_ref[...].T + x_ref[...] * dsk_ref[...]
    ssm_ref[...] = _rms(yy * _silu(z_ref[...]), gs_ref[...]).astype(BF16)
    vn = _rms(v_ref[...], gv_ref[...])
    vn_ref[...] = vn
    cm_ref[...] = (u_ref[...] * (vn * wv_ref[...] + bv_ref[...])).astype(BF16)


def sample_mix(y_t, x, proj, lw):
    nb = x.shape[0]
    const = lambda i: (0, 0)
    blk = lambda col: pl.BlockSpec((nb, D_SSM), lambda i, col=col: (0, col // D_SSM))
    vec = pl.BlockSpec((1, D_SSM), const)
    return pl.pallas_call(
        _sample_mix_kernel,
        out_shape=(jax.ShapeDtypeStruct((nb, D_SSM), BF16), jax.ShapeDtypeStruct((nb, D_CM), BF16),
                   jax.ShapeDtypeStruct((nb, D_CM), F32)),
        grid=(1,),
        in_specs=[pl.BlockSpec((D_SSM, nb), const), pl.BlockSpec((nb, D_SSM), const),
                  blk(COL_Z), blk(COL_U), blk(COL_V), vec, vec, vec, vec, vec],
        out_specs=(pl.BlockSpec((nb, D_SSM), const), pl.BlockSpec((nb, D_CM), const),
                   pl.BlockSpec((nb, D_CM), const)),
        compiler_params=_cparams(("arbitrary",)),
        name="sample_mix",
    )(y_t, x, proj, proj, proj, lw['d_skip'], lw['g_ssm_out'], lw['g_sgu'], lw['sgu_w0'], lw['sgu_b0'])


def _spatial_gate_kernel(u_ref, v_ref, g_ref, w_ref, bt_ref, o_ref, *, n_chunks):
    vn = _rms(v_ref[...], g_ref[...]).astype(BF16)
    ii = lax.broadcasted_iota(jnp.int32, (CHUNK, CHUNK), 0)
    jj = lax.broadcasted_iota(jnp.int32, (CHUNK, CHUNK), 1)
    bt = bt_ref[...]
    for g in range(N_CG):
        w = jnp.where(jj <= ii, w_ref[g], 0.0).astype(BF16)
        bcol = bt[:, g:g + 1]
        for c in range(n_chunks):
            rs = slice(c * CHUNK, (c + 1) * CHUNK)
            cs = slice(g * CG_DIM, (g + 1) * CG_DIM)
            mixed = _dot(w, vn[rs, cs]) + bcol
            o_ref[rs, cs] = (u_ref[rs, cs] * mixed).astype(BF16)


def spatial_gate(proj, lw, *, tm):
    r = proj.shape[0]
    const = lambda i: (0, 0)
    return pl.pallas_call(
        functools.partial(_spatial_gate_kernel, n_chunks=tm // CHUNK),
        out_shape=jax.ShapeDtypeStruct((r, D_CM), BF16),
        grid=(r // tm,),
        in_specs=[pl.BlockSpec((tm, D_CM), lambda i: (i, COL_U // D_CM)),
                  pl.BlockSpec((tm, D_CM), lambda i: (i, COL_V // D_CM)),
                  pl.BlockSpec((1, D_CM), const),
                  pl.BlockSpec((N_CG, CHUNK, CHUNK), lambda i: (0, 0, 0)),
                  pl.BlockSpec((CHUNK, N_CG), const)],
        out_specs=pl.BlockSpec((tm, D_CM), lambda i: (i, 0)),
        compiler_params=_cparams(("parallel",)),
        name="spatial_gate",
    )(proj, proj, lw['g_sgu'], lw['w_spatial'], lw['b_spatial_t'])


def _out_proj_kernel(x_ref, att_ref, ssm_ref, cm_ref, w_ref, o_ref):
    na = N_HEADS_A * V_DIM
    acc = _dot(att_ref[...], w_ref[0:na, :])
    acc = acc + _dot(ssm_ref[...], w_ref[na:na + D_SSM, :])
    acc = acc + _dot(cm_ref[...], w_ref[na + D_SSM:, :])
    o_ref[...] = x_ref[...] + acc


def out_proj(x, att, ssm, cm, w, *, tm, tn):
    r, d = x.shape
    return pl.pallas_call(
        _out_proj_kernel,
        out_shape=jax.ShapeDtypeStruct((r, d), F32),
        grid=(r // tm, d // tn),
        in_specs=[pl.BlockSpec((tm, tn), lambda i, j: (i, j)),
                  pl.BlockSpec((tm, att.shape[1]), lambda i, j: (i, 0)),
                  pl.BlockSpec((tm, ssm.shape[1]), lambda i, j: (i, 0)),
                  pl.BlockSpec((tm, cm.shape[1]), lambda i, j: (i, 0)),
                  pl.BlockSpec((w.shape[0], tn), lambda i, j: (0, j))],
        out_specs=pl.BlockSpec((tm, tn), lambda i, j: (i, j)),
        compiler_params=_cparams(("parallel", "parallel")),
        name="out_proj",
    )(x, att, ssm, cm, w)


FFN_HALO = 16


def _ffn_prompt_kernel(x_ref, xp_ref, g_ref, wg_ref, wu_ref, cw_ref, cb_ref, wd_ref, o_ref, h_ref, gbuf, acc_ref,
                       *, tm, seq):
    i = pl.program_id(0)
    j = pl.program_id(1)

    @pl.when(j == 0)
    def _():
        keep = ((i * tm) % seq != 0).astype(F32)
        h_ref[0:FFN_HALO, :] = (_rms(xp_ref[...], g_ref[...]) * keep).astype(BF16)
        h_ref[FFN_HALO:, :] = _rms(x_ref[...], g_ref[...]).astype(BF16)
        acc_ref[...] = jnp.zeros(acc_ref.shape, F32)

    gbuf[...] = _dot(h_ref[...], wg_ref[...])
    up = _dot(h_ref[FFN_HALO:, :], wu_ref[...])
    cw = cw_ref[...]
    y = cb_ref[...]
    for t in range(FFN_CONV):
        off = FFN_HALO - (FFN_CONV - 1) + t
        y = y + cw[t:t + 1, :] * gbuf[off:off + tm, :]
    acc_ref[...] += _dot((_silu(y) * up).astype(BF16), wd_ref[...])

    @pl.when(j == pl.num_programs(1) - 1)
    def _():
        o_ref[...] = x_ref[...] + acc_ref[...]


def ffn_prompt(x, seq, lw, *, tm, tf):
    r, d = x.shape
    const = lambda i, j: (0, 0)
    halo_blocks = tm // FFN_HALO
    return pl.pallas_call(
        functools.partial(_ffn_prompt_kernel, tm=tm, seq=seq),
        out_shape=jax.ShapeDtypeStruct((r, d), F32),
        grid=(r // tm, D_FF // tf),
        in_specs=[pl.BlockSpec((tm, d), lambda i, j: (i, 0)),
                  pl.BlockSpec((FFN_HALO, d), lambda i, j: (jnp.maximum(i * halo_blocks - 1, 0), 0)),
                  pl.BlockSpec((1, d), const),
                  pl.BlockSpec((d, tf), lambda i, j: (0, j)),
                  pl.BlockSpec((d, tf), lambda i, j: (0, j)),
                  pl.BlockSpec((FFN_CONV, tf), lambda i, j: (0, j)),
                  pl.BlockSpec((1, tf), lambda i, j: (0, j)),
                  pl.BlockSpec((tf, d), lambda i, j: (j, 0))],
        out_specs=pl.BlockSpec((tm, d), lambda i, j: (i, 0)),
        scratch_shapes=[pltpu.VMEM((tm + FFN_HALO, d), BF16), pltpu.VMEM((tm + FFN_HALO, tf), F32),
                        pltpu.VMEM((tm, d), F32)],
        compiler_params=_cparams(("parallel", "arbitrary")),
        name="ffn_prompt",
    )(x, x, lw['g_ffn'], lw['w_gate'], lw['w_up'], lw['ffn_conv_w'], lw['ffn_conv_b'], lw['w_down'])


def _ffn_sample_kernel(x_ref, p0_ref, p1_ref, g_ref, wg_ref, wu_ref, cw_ref, cb_ref, wd_ref, o_ref, gate_ref,
                       h_ref, acc_ref):
    j = pl.program_id(0)

    @pl.when(j == 0)
    def _():
        h_ref[...] = _rms(x_ref[...], g_ref[...]).astype(BF16)
        acc_ref[...] = jnp.zeros(acc_ref.shape, F32)

    gate = _dot(h_ref[...], wg_ref[...])
    up = _dot(h_ref[...], wu_ref[...])
    gate_ref[...] = gate
    cw = cw_ref[...]
    y = cb_ref[...] + cw[0:1] * p0_ref[...] + cw[1:2] * p1_ref[...] + cw[2:3] * gate
    acc_ref[...] += _dot((_silu(y) * up).astype(BF16), wd_ref[...])

    @pl.when(j == pl.num_programs(0) - 1)
    def _():
        o_ref[...] = x_ref[...] + acc_ref[...]


def ffn_sample(x, prev, lw, *, tf):
    nb, d = x.shape
    nf = D_FF // tf
    return pl.pallas_call(
        _ffn_sample_kernel,
        out_shape=(jax.ShapeDtypeStruct((nb, d), F32), jax.ShapeDtypeStruct((nb, D_FF), F32)),
        grid=(nf,),
        in_specs=[pl.BlockSpec((nb, d), lambda j: (0, 0)),
                  pl.BlockSpec((nb, tf), lambda j: (0, j)),
                  pl.BlockSpec((nb, tf), lambda j: (0, nf + j)),
                  pl.BlockSpec((1, d), lambda j: (0, 0)),
                  pl.BlockSpec((d, tf), lambda j: (0, j)),
                  pl.BlockSpec((d, tf), lambda j: (0, j)),
                  pl.BlockSpec((FFN_CONV, tf), lambda j: (0, j)),
                  pl.BlockSpec((1, tf), lambda j: (0, j)),
                  pl.BlockSpec((tf, d), lambda j: (j, 0))],
        out_specs=(pl.BlockSpec((nb, d), lambda j: (0, 0)), pl.BlockSpec((nb, tf), lambda j: (0, j))),
        scratch_shapes=[pltpu.VMEM((nb, d), BF16), pltpu.VMEM((nb, d), F32)],
        compiler_params=_cparams(("arbitrary",)),
        name="ffn_sample",
    )(x, prev, prev, lw['g_ffn'], lw['w_gate'], lw['w_up'], lw['ffn_conv_w'], lw['ffn_conv_b'], lw['w_down'])


def _rope_tables(pos):
    inv = ROPE_THETA ** (-jnp.arange(0, ROPE_DIM, 2, dtype=F32) / ROPE_DIM)
    ang = pos.astype(F32)[:, None] * inv[None, :]
    cos, sin = jnp.cos(ang), jnp.sin(ang)
    zero = jnp.zeros_like(cos)
    return (jnp.concatenate([cos, cos, zero, zero], axis=1),
            jnp.concatenate([-sin, zero, zero, zero], axis=1),
            jnp.concatenate([zero, sin, zero, zero], axis=1))


def _pack_layer(li, p):
    d = D_MODEL
    w_in = p['w_in'][li]
    offs = np.cumsum([0, Q_LORA, KV_LORA, ROPE_DIM, D_SSM, CONV_DIM, N_HEADS_S, D_CM, D_CM])
    seg = lambda k: w_in[:, int(offs[k]):int(offs[k + 1])]
    zpad = lambda n: jnp.zeros((d, n), w_in.dtype)
    w_in_p = jnp.concatenate([seg(0), seg(3), seg(6), seg(7), seg(4), seg(1),
                              seg(2), zpad(LANE - ROPE_DIM), seg(5), zpad(LANE - N_HEADS_S)], axis=1).astype(BF16)
    h = N_HEADS_A
    w_uq = jnp.pad(p['w_uq'][li], ((0, 0), (0, 0), (0, QK_PAD - QK_DIM))).reshape(Q_LORA, h * QK_PAD).astype(BF16)
    w_uk = p['w_uk'][li].reshape(KV_LORA, h * NOPE_DIM).astype(BF16)
    w_uv = p['w_uv'][li].reshape(KV_LORA, h * V_DIM).astype(BF16)
    pad_g = lambda g: jnp.pad(g, (0, QK_PAD - QK_DIM))[None, :]
    row = lambda v: v[None, :]
    lane_pad = lambda v: jnp.pad(v, (0, LANE - v.shape[0]))[None, :]
    head_expand = np.zeros((LANE, D_SSM), np.float32)
    head_expand[:N_HEADS_S] = np.kron(np.eye(N_HEADS_S, dtype=np.float32), np.ones((1, SSM_HEAD_DIM), np.float32))
    head_expand = jnp.asarray(head_expand)
    w_gu = p['w_gate_up'][li]
    return {
        'g_mix': row(p['g_mix'][li]), 'w_in': w_in_p,
        'g_q_a': row(p['g_q_a'][li]), 'g_kv_a': row(p['g_kv_a'][li]),
        'w_uq': w_uq, 'w_uk': w_uk, 'w_uk_t': w_uk.T, 'w_uv': w_uv,
        'gq': pad_g(p['g_qk_q'][li]) * (QK_DIM ** -0.5),
        'gk': pad_g(p['g_qk_k'][li]),
        'ssm_conv_w': p['ssm_conv_w'][li], 'ssm_conv_b': row(p['ssm_conv_b'][li]),
        'dt_bias': lane_pad(p['ssm_dt_bias'][li]), 'a_log': lane_pad(p['ssm_a_log'][li]),
        'd_skip': row(jnp.repeat(p['ssm_d'][li], SSM_HEAD_DIM)),
        'g_ssm_out': row(p['g_ssm_out'][li]), 'head_expand': head_expand,
        'g_sgu': row(p['g_sgu'][li]), 'w_spatial': p['w_spatial'][li], 'b_spatial_t': p['b_spatial'][li].T,
        'sgu_w0': row(jnp.repeat(p['w_spatial'][li][:, 0, 0], CG_DIM)),
        'sgu_b0': row(jnp.repeat(p['b_spatial'][li][:, 0], CG_DIM)),
        'w_out': p['w_out'][li].astype(BF16),
        'g_ffn': row(p['g_ffn'][li]),
        'w_gate': w_gu[:, :D_FF].astype(BF16), 'w_up': w_gu[:, D_FF:].astype(BF16),
        'ffn_conv_w': p['ffn_conv_w'][li], 'ffn_conv_b': row(p['ffn_conv_b'][li]),
        'w_down': p['w_down'][li].astype(BF16),
    }


def _tile(n, pref):
    t = min(n, pref)
    assert n % t == 0, (n, pref)
    return t


def kernel(x_prompt, x_sample, cache_latent, cache_krope, state_ssm, state_conv, state_ffn_conv, page_table, g_mix, w_in, g_q_a, w_uq, g_kv_a, w_uk, w_uv, g_qk_q, g_qk_k, ssm_conv_w, ssm_conv_b, ssm_dt_bias, ssm_a_log, ssm_d, g_ssm_out, g_sgu, w_spatial, b_spatial, w_out, g_ffn, w_gate_up, ffn_conv_w, ffn_conv_b, w_down):
    params = dict(g_mix=g_mix, w_in=w_in, g_q_a=g_q_a, w_uq=w_uq, g_kv_a=g_kv_a, w_uk=w_uk, w_uv=w_uv,
                  g_qk_q=g_qk_q, g_qk_k=g_qk_k, ssm_conv_w=ssm_conv_w, ssm_conv_b=ssm_conv_b,
                  ssm_dt_bias=ssm_dt_bias, ssm_a_log=ssm_a_log, ssm_d=ssm_d, g_ssm_out=g_ssm_out, g_sgu=g_sgu,
                  w_spatial=w_spatial, b_spatial=b_spatial, w_out=w_out, g_ffn=g_ffn, w_gate_up=w_gate_up,
                  ffn_conv_w=ffn_conv_w, ffn_conv_b=ffn_conv_b, w_down=w_down)
    depth = w_in.shape[0]
    bp, seq, d = x_prompt.shape
    bd, dec_seq, _ = x_sample.shape
    assert dec_seq == 1 and d == D_MODEL and cache_latent.shape[2] == PAGE
    n_pages = page_table.shape[1]
    past_len = n_pages * PAGE
    rp = bp * seq

    tabs_p = _rope_tables(jnp.arange(seq, dtype=jnp.int32))
    tabs_s = _rope_tables(jnp.full((bd,), past_len, jnp.int32))

    tm_p = _tile(seq, 512)
    tq = _tile(seq, 1024)
    tk = tq
    pp = next(c for c in (16, 8, 4, 2) if n_pages % c == 0)
    cache_krope_t = jnp.swapaxes(cache_krope, 2, 3)
    tf = 512

    yp = x_prompt.reshape(rp, d)
    ys = x_sample.reshape(bd, d)
    outs = [[] for _ in range(11)]
    for li in range(depth):
        lw = _pack_layer(li, params)

        proj = norm_matmul(yp, lw['g_mix'], lw['w_in'], tm=tm_p, tn=512)
        q, k, v, lat, kro = mla_prep(proj, bp, seq, lw, lw['gq'] * LOG2E, tabs_p, tm=_tile(seq, 256), q_dtype=BF16)
        att = flash_attention(q, k, v, tq=tq, tk=tk)
        ssm, ssm_state = ssd_prompt(proj, bp, seq, lw)
        cm = spatial_gate(proj, lw, tm=tm_p)
        y_mid = out_proj(yp, att.reshape(rp, -1), ssm, cm, lw['w_out'], tm=tm_p, tn=512)
        yp = ffn_prompt(y_mid, seq, lw, tm=tm_p, tf=tf)
        tail = y_mid.reshape(bp, seq, d)[:, seq - (FFN_CONV - 1):, :].reshape(bp * (FFN_CONV - 1), d)
        tail = jnp.pad(tail, ((0, 8 - tail.shape[0]), (0, 0)))
        gate_tail = norm_matmul(tail, lw['g_ffn'], lw['w_gate'], tm=8, tn=tf)[:bp * (FFN_CONV - 1)]
        proj3 = proj.reshape(bp, seq, IN_PAD)
        outs[0].append(lat)
        outs[1].append(kro)
        outs[2].append(ssm_state)
        outs[3].append(proj3[:, seq - (SSM_CONV - 1):, COL_XBC:COL_XBC + CONV_DIM])
        outs[4].append(gate_tail.reshape(bp, FFN_CONV - 1, D_FF))

        proj_s = norm_matmul(ys, lw['g_mix'], lw['w_in'], tm=bd, tn=512)
        q_s, k_s, _, lat_s, kro_s = mla_prep(proj_s, 1, bd, lw, lw['gq'], tabs_s, tm=bd, q_dtype=F32)
        qabs, qgr = dec_q(q_s, lw['gk'], lw['w_uk'])
        m, l, acc = dec_paged(li, page_table, cache_latent, cache_krope_t, lw['w_uk_t'],
                              qabs.transpose(1, 0, 2), qgr.transpose(1, 0, 2), pp=pp)
        att_s = dec_finish(m.transpose(1, 0, 2), l.transpose(1, 0, 2), acc.transpose(1, 0, 2), q_s, k_s,
                           lat_s[0], lw['w_uv'])
        conv_prev = state_conv[li]
        x_s, bc_s, xdt_t, dec_t = ssd_sample_prep(proj_s, conv_prev.reshape(bd, -1), lw)
        ssm_state_s, y_t = ssd_sample_state(li, state_ssm, xdt_t, dec_t, bc_s)
        ssm_s, cm_s, vn_s = sample_mix(y_t, x_s, proj_s, lw)
        ys_mid = out_proj(ys, att_s, ssm_s, cm_s, lw['w_out'], tm=bd, tn=512)
        ffn_prev = state_ffn_conv[li]
        ys, gate_s = ffn_sample(ys_mid, ffn_prev.reshape(bd, -1), lw, tf=tf)
        outs[5].append(lat_s.reshape(bd, 1, KV_LORA))
        outs[6].append(kro_s.reshape(bd, 1, ROPE_DIM))
        outs[7].append(ssm_state_s)
        outs[8].append(jnp.concatenate([conv_prev[:, 1:], proj_s[:, None, COL_XBC:COL_XBC + CONV_DIM]], axis=1))
        outs[9].append(jnp.concatenate([ffn_prev[:, 1:], gate_s[:, None, :]], axis=1))
        outs[10].append(vn_s.reshape(bd, 1, D_CM))

    stacked = [jnp.stack(o) for o in outs]
    return (yp.reshape(bp, seq, d), ys.reshape(bd, 1, d), *stacked)
```

```python
import functools

import numpy as np
import jax
import jax.numpy as jnp
from jax import lax
from jax.experimental import pallas as pl
from jax.experimental.pallas import tpu as pltpu

F32 = jnp.float32
BF16 = jnp.bfloat16
EPS = 1e-6
HIGHEST = lax.Precision.HIGHEST

D_MODEL = 2048
N_HEADS_A = 8
NOPE_DIM = 128
ROPE_DIM = 64
QK_DIM = NOPE_DIM + ROPE_DIM
QK_PAD = 256
V_DIM = 128
Q_LORA = 512
KV_LORA = 256
ROPE_THETA = 10000.0
D_SSM = 512
SSM_HEAD_DIM = 64
N_HEADS_S = 8
N_GROUPS_S = 2
D_STATE = 128
SSM_CONV = 4
SSD_CHUNK = 128
CONV_DIM = 1024
D_CM = 512
CHUNK = 128
N_CG = 4
CG_DIM = 128
D_FF = 5632
FFN_CONV = 3
PAGE = 128

COL_CQ, COL_Z, COL_U, COL_V, COL_XBC, COL_CKV, COL_KR, COL_DT = 0, 512, 1024, 1536, 2048, 3072, 3328, 3456
IN_PAD = 3584

LANE = 128
VMEM_LIMIT = 56 * 1024 * 1024

NT_DIMS = (((1,), (1,)), ((), ()))


def _cparams(sem):
    return pltpu.CompilerParams(dimension_semantics=sem, vmem_limit_bytes=VMEM_LIMIT)


def _rms(x, g):
    return x * lax.rsqrt(jnp.mean(x * x, axis=-1, keepdims=True) + EPS) * g


def _silu(x):
    return x * (1.0 / (1.0 + jnp.exp(-x)))


def _softplus(x):
    return jnp.maximum(x, 0.0) + jnp.log(1.0 + jnp.exp(-jnp.abs(x)))


def _dot(a, b):
    return jnp.dot(a, b, preferred_element_type=F32)


def _dot_nt(a, b):
    return lax.dot_general(a, b, NT_DIMS, preferred_element_type=F32)


def _norm_matmul_kernel(x_ref, g_ref, w_ref, o_ref, h_ref):
    @pl.when(pl.program_id(1) == 0)
    def _():
        h_ref[...] = _rms(x_ref[...], g_ref[...]).astype(BF16)

    o_ref[...] = _dot(h_ref[...], w_ref[...])


def norm_matmul(x, g, w, *, tm, tn):
    r, k = x.shape
    n = w.shape[1]
    w_mode = dict(pipeline_mode=pl.Buffered(1)) if tn == n else {}
    return pl.pallas_call(
        _norm_matmul_kernel,
        out_shape=jax.ShapeDtypeStruct((r, n), F32),
        grid=(r // tm, n // tn),
        in_specs=[pl.BlockSpec((tm, k), lambda i, j: (i, 0)),
                  pl.BlockSpec((1, k), lambda i, j: (0, 0)),
                  pl.BlockSpec((k, tn), lambda i, j: (0, j), **w_mode)],
        out_specs=pl.BlockSpec((tm, tn), lambda i, j: (i, j)),
        scratch_shapes=[pltpu.VMEM((tm, k), BF16)],
        compiler_params=_cparams(("parallel", "arbitrary")),
        name="norm_matmul",
    )(x, g, w)


def _mla_prep_kernel(cq_ref, ckv_ref, kr_ref, gqa_ref, gkva_ref, wuq_ref, wuk_ref, wuv_ref, gq_ref, gk_ref,
                     cos_ref, s1_ref, s2_ref, q_ref, k_ref, v_ref, lat_ref, kro_ref):
    cos = cos_ref[...]
    s1 = s1_ref[...]
    s2 = s2_ref[...]

    def rope(b):
        return b * cos + pltpu.roll(b, 96, 1) * s1 + pltpu.roll(b, 32, 1) * s2

    def head_norm(a, b, g):
        ss = jnp.sum(a * a + b * b, axis=-1, keepdims=True)
        inv = lax.rsqrt(ss * (1.0 / QK_DIM) + EPS)
        return a * inv * g[:, :NOPE_DIM], b * inv * g[:, NOPE_DIM:]

    cqn = _rms(cq_ref[...], gqa_ref[...]).astype(BF16)
    qf = _dot(cqn, wuq_ref[...])
    gq = gq_ref[...]
    for h in range(N_HEADS_A):
        a = qf[:, h * QK_PAD:h * QK_PAD + NOPE_DIM]
        b = rope(qf[:, h * QK_PAD + NOPE_DIM:(h + 1) * QK_PAD])
        a, b = head_norm(a, b, gq)
        q_ref[0, h, :, 0:NOPE_DIM] = a.astype(q_ref.dtype)
        q_ref[0, h, :, NOPE_DIM:QK_PAD] = b.astype(q_ref.dtype)

    ckvn = _rms(ckv_ref[...], gkva_ref[...])
    lat_ref[0] = ckvn
    cb = ckvn.astype(BF16)
    kn = _dot(cb, wuk_ref[...])
    vf = _dot(cb, wuv_ref[...])
    krr = rope(kr_ref[...])
    kro_ref[0] = krr[:, :ROPE_DIM]
    gk = gk_ref[...]
    for h in range(N_HEADS_A):
        a, b = head_norm(kn[:, h * NOPE_DIM:(h + 1) * NOPE_DIM], krr, gk)
        k_ref[0, h, :, 0:NOPE_DIM] = a.astype(BF16)
        k_ref[0, h, :, NOPE_DIM:QK_PAD] = b.astype(BF16)
        v_ref[0, h] = vf[:, h * V_DIM:(h + 1) * V_DIM].astype(BF16)


def mla_prep(proj, nb, seq, lw, gq, tabs, *, tm, q_dtype):
    nt = seq // tm
    row = lambda b, t: (b * nt + t, 0)
    const = lambda b, t: (0, 0)
    tab = lambda b, t: (t, 0)
    h = N_HEADS_A
    return pl.pallas_call(
        _mla_prep_kernel,
        out_shape=(jax.ShapeDtypeStruct((nb, h, seq, QK_PAD), q_dtype),
                   jax.ShapeDtypeStruct((nb, h, seq, QK_PAD), BF16),
                   jax.ShapeDtypeStruct((nb, h, seq, V_DIM), BF16),
                   jax.ShapeDtypeStruct((nb, seq, KV_LORA), F32),
                   jax.ShapeDtypeStruct((nb, seq, ROPE_DIM), F32)),
        grid=(nb, nt),
        in_specs=[pl.BlockSpec((tm, Q_LORA), lambda b, t: (b * nt + t, COL_CQ // Q_LORA)),
                  pl.BlockSpec((tm, KV_LORA), lambda b, t: (b * nt + t, COL_CKV // KV_LORA)),
                  pl.BlockSpec((tm, LANE), lambda b, t: (b * nt + t, COL_KR // LANE)),
                  pl.BlockSpec((1, Q_LORA), const),
                  pl.BlockSpec((1, KV_LORA), const),
                  pl.BlockSpec((Q_LORA, h * QK_PAD), const),
                  pl.BlockSpec((KV_LORA, h * NOPE_DIM), const),
                  pl.BlockSpec((KV_LORA, h * V_DIM), const),
                  pl.BlockSpec((1, QK_PAD), const),
                  pl.BlockSpec((1, QK_PAD), const),
                  pl.BlockSpec((tm, LANE), tab),
                  pl.BlockSpec((tm, LANE), tab),
                  pl.BlockSpec((tm, LANE), tab)],
        out_specs=(pl.BlockSpec((1, h, tm, QK_PAD), lambda b, t: (b, 0, t, 0)),
                   pl.BlockSpec((1, h, tm, QK_PAD), lambda b, t: (b, 0, t, 0)),
                   pl.BlockSpec((1, h, tm, V_DIM), lambda b, t: (b, 0, t, 0)),
                   pl.BlockSpec((1, tm, KV_LORA), lambda b, t: (b, t, 0)),
                   pl.BlockSpec((1, tm, ROPE_DIM), lambda b, t: (b, t, 0))),
        compiler_params=_cparams(("parallel", "parallel")),
        name="mla_prep",
    )(proj, proj, proj, lw['g_q_a'], lw['g_kv_a'], lw['w_uq'], lw['w_uk'], lw['w_uv'], gq, lw['gk'],
      tabs[0], tabs[1], tabs[2])


FLASH_ROWS = 256
LOG2E = 1.4426950408889634


def _flash_kernel(qi_ref, ki_ref, q_ref, k_ref, v_ref, o_ref, m_ref, l_ref, acc_ref, *, tq, tk):
    p = pl.program_id(2)
    qi = qi_ref[p]
    ki = ki_ref[p]

    @pl.when(ki == 0)
    def _():
        m_ref[...] = jnp.full(m_ref.shape, -jnp.inf, F32)
        l_ref[...] = jnp.zeros(l_ref.shape, F32)
        acc_ref[...] = jnp.zeros(acc_ref.shape, F32)

    def update(r, n_cols, masked):
        rows = slice(r * FLASH_ROWS, (r + 1) * FLASH_ROWS)
        s = _dot_nt(q_ref[0, 0, rows, :], k_ref[0, 0, 0:n_cols, :])
        if masked:
            row = r * FLASH_ROWS + lax.broadcasted_iota(jnp.int32, s.shape, 0)
            col = lax.broadcasted_iota(jnp.int32, s.shape, 1)
            s = jnp.where(col <= row, s, -jnp.inf)
        m_prev = m_ref[rows, :]
        m_new = jnp.maximum(m_prev, jnp.max(s, axis=-1, keepdims=True))
        alpha = jnp.exp2(m_prev - m_new)
        pr = jnp.exp2(s - m_new)
        l_ref[rows, :] = alpha * l_ref[rows, :] + jnp.sum(pr, axis=-1, keepdims=True)
        acc_ref[rows, :] = alpha * acc_ref[rows, :] + _dot(pr.astype(BF16), v_ref[0, 0, 0:n_cols, :])
        m_ref[rows, :] = m_new

    @pl.when(ki < qi)
    def _():
        for r in range(tq // FLASH_ROWS):
            update(r, tk, False)

    @pl.when(ki == qi)
    def _():
        for r in range(tq // FLASH_ROWS):
            update(r, (r + 1) * FLASH_ROWS, True)
        o_ref[0] = (acc_ref[...] / l_ref[...]).astype(BF16)


def flash_attention(q, k, v, *, tq, tk):
    b, h, s, _ = q.shape
    assert tq == tk and tq % FLASH_ROWS == 0
    pairs = [(i, j) for i in range(s // tq) for j in range(i + 1)]
    qi_tab = jnp.asarray(np.array([p[0] for p in pairs], np.int32))
    ki_tab = jnp.asarray(np.array([p[1] for p in pairs], np.int32))
    return pl.pallas_call(
        functools.partial(_flash_kernel, tq=tq, tk=tk),
        out_shape=jax.ShapeDtypeStruct((b, s, h * V_DIM), BF16),
        grid_spec=pltpu.PrefetchScalarGridSpec(
            num_scalar_prefetch=2,
            grid=(b, h, len(pairs)),
            in_specs=[pl.BlockSpec((1, 1, tq, QK_PAD), lambda bi, hi, p, qt, kt: (bi, hi, qt[p], 0)),
                      pl.BlockSpec((1, 1, tk, QK_PAD), lambda bi, hi, p, qt, kt: (bi, hi, kt[p], 0)),
                      pl.BlockSpec((1, 1, tk, V_DIM), lambda bi, hi, p, qt, kt: (bi, hi, kt[p], 0))],
            out_specs=pl.BlockSpec((1, tq, V_DIM), lambda bi, hi, p, qt, kt: (bi, qt[p], hi)),
            scratch_shapes=[pltpu.VMEM((tq, 1), F32), pltpu.VMEM((tq, 1), F32), pltpu.VMEM((tq, V_DIM), F32)]),
        compiler_params=_cparams(("parallel", "parallel", "arbitrary")),
        name="flash_attention",
    )(qi_tab, ki_tab, q, k, v)


def _dec_q_kernel(q_ref, gk_ref, wuk_ref, qabs_ref, qgr_ref):
    gk = gk_ref[...]
    nb = q_ref.shape[2]
    for h in range(N_HEADS_A):
        qg = q_ref[0, h] * gk
        qn = qg[:, :NOPE_DIM].astype(BF16)
        qabs_ref[h] = _dot_nt(qn, wuk_ref[:, h * NOPE_DIM:(h + 1) * NOPE_DIM]).astype(BF16)
        qgr_ref[h] = qg[:, NOPE_DIM:QK_DIM].astype(BF16)
    for h in range(N_HEADS_A, 2 * N_HEADS_A):
        qabs_ref[h] = jnp.zeros((nb, KV_LORA), BF16)
        qgr_ref[h] = jnp.zeros((nb, ROPE_DIM), BF16)


def dec_q(q, gk, wuk):
    nb = q.shape[2]
    return pl.pallas_call(
        _dec_q_kernel,
        out_shape=(jax.ShapeDtypeStruct((2 * N_HEADS_A, nb, KV_LORA), BF16),
                   jax.ShapeDtypeStruct((2 * N_HEADS_A, nb, ROPE_DIM), BF16)),
        compiler_params=pltpu.CompilerParams(vmem_limit_bytes=VMEM_LIMIT),
        name="dec_q",
    )(q, gk, wuk)


def _dec_paged_kernel(pt_ref, lat_hbm, krt_hbm, wukt_ref, qabs_ref, qgr_ref, m_out, l_out, acc_out,
                      latbuf, krbuf, sem, lhs_ref, cbuf, m_s, l_s, acc_s, *, li, pp):
    j = pl.program_id(1)
    nj = pl.num_programs(1)
    step = pl.program_id(0) * nj + j
    last_step = pl.num_programs(0) * nj - 1
    slot = step % 2
    nh = N_HEADS_A
    n_k = nh * NOPE_DIM

    def page_copies(st, sl):
        row = st // nj
        col = (st % nj) * pp
        copies = []
        for i in range(pp):
            pid = pt_ref[row, col + i]
            copies.append(pltpu.make_async_copy(lat_hbm.at[li, pid], latbuf.at[sl, i], sem.at[0, sl]))
            copies.append(pltpu.make_async_copy(krt_hbm.at[li, pid], krbuf.at[sl, i], sem.at[1, sl]))
        return copies

    @pl.when(step == 0)
    def _():
        for c in page_copies(step, slot):
            c.start()

    @pl.when(step < last_step)
    def _():
        for c in page_copies(step + 1, 1 - slot):
            c.start()

    @pl.when(j == 0)
    def _():
        lhs_ref[0:n_k, :] = wukt_ref[...]
        lhs_ref[n_k:n_k + 2 * nh, :] = qabs_ref[0]
        m_s[...] = jnp.full(m_s.shape, -jnp.inf, F32)
        l_s[...] = jnp.zeros(l_s.shape, F32)
        acc_s[...] = jnp.zeros(acc_s.shape, F32)

    for c in page_copies(step, slot):
        c.wait()

    qgr = qgr_ref[0]
    parts = []
    for t in range(pp // 2):
        cb = jnp.concatenate([latbuf[slot, 2 * t], latbuf[slot, 2 * t + 1]], axis=0).astype(BF16)
        cbuf[t * 2 * PAGE:(t + 1) * 2 * PAGE, :] = cb
        kt = _dot_nt(lhs_ref[...], cb)
        k3 = kt[:n_k].reshape(nh, NOPE_DIM, 2 * PAGE)
        ssq = jnp.sum(k3 * k3, axis=1)
        krt = jnp.concatenate([krbuf[slot, 2 * t], krbuf[slot, 2 * t + 1]], axis=1)
        ssq = ssq + jnp.sum(krt * krt, axis=0, keepdims=True)
        num = kt[n_k:n_k + nh] + _dot(qgr, krt.astype(BF16))[:nh]
        parts.append(num * lax.rsqrt(ssq * (1.0 / QK_DIM) + EPS))
    s = jnp.concatenate(parts, axis=1)
    m_prev = m_s[...]
    m_new = jnp.maximum(m_prev, jnp.max(s, axis=-1, keepdims=True))
    alpha = jnp.exp(m_prev - m_new)
    p = jnp.exp(s - m_new)
    l_s[...] = alpha * l_s[...] + jnp.sum(p, axis=-1, keepdims=True)
    pb = jnp.concatenate([p, jnp.zeros_like(p)], axis=0).astype(BF16)
    acc_s[...] = alpha * acc_s[...] + _dot(pb, cbuf[...])[:nh]
    m_s[...] = m_new

    @pl.when(j == pl.num_programs(1) - 1)
    def _():
        m_out[0] = jnp.broadcast_to(m_s[...], (nh, LANE))
        l_out[0] = jnp.broadcast_to(l_s[...], (nh, LANE))
        acc_out[0] = acc_s[...]


def dec_paged(li, page_table, cache_lat, cache_kr_t, wukt, qabs, qgr, *, pp):
    nb, n_pages = page_table.shape
    nh = N_HEADS_A
    return pl.pallas_call(
        functools.partial(_dec_paged_kernel, li=li, pp=pp),
        out_shape=(jax.ShapeDtypeStruct((nb, nh, LANE), F32),
                   jax.ShapeDtypeStruct((nb, nh, LANE), F32),
                   jax.ShapeDtypeStruct((nb, nh, KV_LORA), F32)),
        grid_spec=pltpu.PrefetchScalarGridSpec(
            num_scalar_prefetch=1,
            grid=(nb, n_pages // pp),
            in_specs=[
                pl.BlockSpec(memory_space=pl.ANY),
                pl.BlockSpec(memory_space=pl.ANY),
                pl.BlockSpec((nh * NOPE_DIM, KV_LORA), lambda b, j, pt: (0, 0)),
                pl.BlockSpec((1, 2 * nh, KV_LORA), lambda b, j, pt: (b, 0, 0)),
                pl.BlockSpec((1, 2 * nh, ROPE_DIM), lambda b, j, pt: (b, 0, 0))],
            out_specs=(pl.BlockSpec((1, nh, LANE), lambda b, j, pt: (b, 0, 0)),
                       pl.BlockSpec((1, nh, LANE), lambda b, j, pt: (b, 0, 0)),
                       pl.BlockSpec((1, nh, KV_LORA), lambda b, j, pt: (b, 0, 0))),
            scratch_shapes=[pltpu.VMEM((2, pp, PAGE, KV_LORA), F32),
                            pltpu.VMEM((2, pp, ROPE_DIM, PAGE), F32),
                            pltpu.SemaphoreType.DMA((2, 2)),
                            pltpu.VMEM((nh * NOPE_DIM + 2 * nh, KV_LORA), BF16),
                            pltpu.VMEM((pp * PAGE, KV_LORA), BF16),
                            pltpu.VMEM((nh, 1), F32), pltpu.VMEM((nh, 1), F32), pltpu.VMEM((nh, KV_LORA), F32)]),
        compiler_params=_cparams(("arbitrary", "arbitrary")),
        name="dec_paged",
    )(page_table, cache_lat, cache_kr_t, wukt, qabs, qgr)


def _dec_finish_kernel(m_ref, l_ref, acc_ref, q_ref, k_ref, lat_ref, wuv_ref, o_ref):
    cn = lat_ref[...].astype(BF16).astype(F32)
    for h in range(N_HEADS_A):
        qh = q_ref[0, h].astype(BF16).astype(F32)
        s_new = jnp.sum(qh * k_ref[0, h].astype(F32), axis=-1, keepdims=True)
        m_h = m_ref[h][:, 0:1]
        l_h = l_ref[h][:, 0:1]
        m_tot = jnp.maximum(m_h, s_new)
        a = jnp.exp(m_h - m_tot)
        pn = jnp.exp(s_new - m_tot)
        lat = (acc_ref[h] * a + pn * cn) / (l_h * a + pn)
        o_ref[:, h * V_DIM:(h + 1) * V_DIM] = _dot(lat.astype(BF16), wuv_ref[:, h * V_DIM:(h + 1) * V_DIM]).astype(BF16)


def dec_finish(m, l, acc, q, k, lat, wuv):
    nb = lat.shape[0]
    return pl.pallas_call(
        _dec_finish_kernel,
        out_shape=jax.ShapeDtypeStruct((nb, N_HEADS_A * V_DIM), BF16),
        compiler_params=pltpu.CompilerParams(vmem_limit_bytes=VMEM_LIMIT),
        name="dec_finish",
    )(m, l, acc, q, k, lat, wuv)


def _ssd_prompt_kernel(xbc_ref, dt_ref, z_ref, cw_ref, cb_ref, dtb_ref, alog_ref, dsk_ref, g_ref, exp_ref,
                       y_ref, st_ref, xbuf, state):
    c = pl.program_id(1)
    lc = SSD_CHUNK
    hd = SSM_HEAD_DIM

    @pl.when(c == 0)
    def _():
        xbuf[0:8, :] = jnp.zeros((8, CONV_DIM), F32)
        state[...] = jnp.zeros(state.shape, F32)

    xbuf[8:8 + lc, :] = xbc_ref[...]
    cw = cw_ref[...]
    y = cb_ref[...]
    for i in range(SSM_CONV):
        y = y + cw[i:i + 1, :] * xbuf[8 - (SSM_CONV - 1) + i:8 - (SSM_CONV - 1) + i + lc, :]
    xbuf[0:8, :] = xbc_ref[lc - 8:lc, :]
    xc = _silu(y)
    x = xc[:, :D_SSM]
    dt = _softplus(dt_ref[...] + dtb_ref[...])
    da = dt * (-jnp.exp(alog_ref[...]))
    ii = lax.broadcasted_iota(jnp.int32, (lc, lc), 0)
    jj = lax.broadcasted_iota(jnp.int32, (lc, lc), 1)
    causal = jj <= ii
    acs = jnp.dot(causal.astype(F32), da, precision=HIGHEST, preferred_element_type=F32)
    acs_t = acs.T
    xdt = x * jnp.dot(dt, exp_ref[...], precision=HIGHEST, preferred_element_type=F32)
    xdt_t = xdt.T.astype(BF16)
    xdt_b = xdt.astype(BF16)
    ys = []
    for g in range(N_GROUPS_S):
        bm = xc[:, D_SSM + g * D_STATE:D_SSM + (g + 1) * D_STATE]
        cm = xc[:, D_SSM + (N_GROUPS_S + g) * D_STATE:D_SSM + (N_GROUPS_S + g + 1) * D_STATE]
        cmb = cm.astype(BF16)
        gmat = _dot_nt(cmb, bm.astype(BF16))
        for hh in range(N_HEADS_S // N_GROUPS_S):
            h = g * (N_HEADS_S // N_GROUPS_S) + hh
            a_col = acs[:, h:h + 1]
            a_row = acs_t[h:h + 1, :]
            a_last = acs[lc - 1:lc, h:h + 1]
            lmat = jnp.exp(jnp.where(causal, a_col - a_row, -jnp.inf))
            y_diag = _dot((gmat * lmat).astype(BF16), xdt_b[:, h * hd:(h + 1) * hd])
            st_prev = state[h]
            y_off = _dot_nt(cmb, st_prev.astype(BF16)) * jnp.exp(a_col)
            bdec = (bm * jnp.exp(a_last - a_col)).astype(BF16)
            state[h] = jnp.exp(a_last) * st_prev + _dot(xdt_t[h * hd:(h + 1) * hd, :], bdec)
            ys.append(y_diag + y_off)
    yy = jnp.concatenate(ys, axis=1) + x * dsk_ref[...]
    y_ref[...] = _rms(yy * _silu(z_ref[...]), g_ref[...]).astype(BF16)

    @pl.when(c == pl.num_programs(1) - 1)
    def _():
        st_ref[0] = state[...]


def ssd_prompt(proj, nb, seq, lw):
    nc = seq // SSD_CHUNK
    lc = SSD_CHUNK
    const = lambda b, c: (0, 0)
    return pl.pallas_call(
        _ssd_prompt_kernel,
        out_shape=(jax.ShapeDtypeStruct((nb * seq, D_SSM), BF16),
                   jax.ShapeDtypeStruct((nb, N_HEADS_S, SSM_HEAD_DIM, D_STATE), F32)),
        grid=(nb, nc),
        in_specs=[pl.BlockSpec((lc, CONV_DIM), lambda b, c: (b * nc + c, COL_XBC // CONV_DIM)),
                  pl.BlockSpec((lc, LANE), lambda b, c: (b * nc + c, COL_DT // LANE)),
                  pl.BlockSpec((lc, D_SSM), lambda b, c: (b * nc + c, COL_Z // D_SSM)),
                  pl.BlockSpec((SSM_CONV, CONV_DIM), const),
                  pl.BlockSpec((1, CONV_DIM), const),
                  pl.BlockSpec((1, LANE), const),
                  pl.BlockSpec((1, LANE), const),
                  pl.BlockSpec((1, D_SSM), const),
                  pl.BlockSpec((1, D_SSM), const),
                  pl.BlockSpec((LANE, D_SSM), const)],
        out_specs=(pl.BlockSpec((lc, D_SSM), lambda b, c: (b * nc + c, 0)),
                   pl.BlockSpec((1, N_HEADS_S, SSM_HEAD_DIM, D_STATE), lambda b, c: (b, 0, 0, 0))),
        scratch_shapes=[pltpu.VMEM((8 + lc, CONV_DIM), F32),
                        pltpu.VMEM((N_HEADS_S, SSM_HEAD_DIM, D_STATE), F32)],
        compiler_params=_cparams(("parallel", "arbitrary")),
        name="ssd_prompt",
    )(proj, proj, proj, lw['ssm_conv_w'], lw['ssm_conv_b'], lw['dt_bias'], lw['a_log'], lw['d_skip'],
      lw['g_ssm_out'], lw['head_expand'])


def _ssd_sample_prep_kernel(xbc_ref, p0_ref, p1_ref, p2_ref, dt_ref, cw_ref, cb_ref, dtb_ref, alog_ref, exp_ref,
                            x_ref, bc_ref, xdt_t_ref, dec_t_ref):
    cw = cw_ref[...]
    y = (cb_ref[...] + cw[0:1] * p0_ref[...] + cw[1:2] * p1_ref[...] + cw[2:3] * p2_ref[...]
         + cw[3:4] * xbc_ref[...])
    xc = _silu(y)
    x = xc[:, :D_SSM]
    dt = _softplus(dt_ref[...] + dtb_ref[...])
    dec = jnp.exp(dt * (-jnp.exp(alog_ref[...])))
    e = exp_ref[...]
    x_ref[...] = x
    bc_ref[...] = xc[:, D_SSM:]
    xdt_t_ref[...] = (x * jnp.dot(dt, e, precision=HIGHEST, preferred_element_type=F32)).T
    dec_t_ref[...] = jnp.dot(dec, e, precision=HIGHEST, preferred_element_type=F32).T


def ssd_sample_prep(proj, conv_state, lw):
    nb = proj.shape[0]
    const = lambda i: (0, 0)
    return pl.pallas_call(
        _ssd_sample_prep_kernel,
        out_shape=(jax.ShapeDtypeStruct((nb, D_SSM), F32),
                   jax.ShapeDtypeStruct((nb, 2 * N_GROUPS_S * D_STATE), F32),
                   jax.ShapeDtypeStruct((D_SSM, nb), F32),
                   jax.ShapeDtypeStruct((D_SSM, nb), F32)),
        grid=(1,),
        in_specs=[pl.BlockSpec((nb, CONV_DIM), lambda i: (0, COL_XBC // CONV_DIM)),
                  pl.BlockSpec((nb, CONV_DIM), lambda i: (0, 0)),
                  pl.BlockSpec((nb, CONV_DIM), lambda i: (0, 1)),
                  pl.BlockSpec((nb, CONV_DIM), lambda i: (0, 2)),
                  pl.BlockSpec((nb, LANE), lambda i: (0, COL_DT // LANE)),
                  pl.BlockSpec((SSM_CONV, CONV_DIM), const),
                  pl.BlockSpec((1, CONV_DIM), const),
                  pl.BlockSpec((1, LANE), const),
                  pl.BlockSpec((1, LANE), const),
                  pl.BlockSpec((LANE, D_SSM), const)],
        out_specs=(pl.BlockSpec((nb, D_SSM), const),
                   pl.BlockSpec((nb, 2 * N_GROUPS_S * D_STATE), const),
                   pl.BlockSpec((D_SSM, nb), const),
                   pl.BlockSpec((D_SSM, nb), const)),
        compiler_params=_cparams(("arbitrary",)),
        name="ssd_sample_prep",
    )(proj, conv_state, conv_state, conv_state, proj, lw['ssm_conv_w'], lw['ssm_conv_b'], lw['dt_bias'],
      lw['a_log'], lw['head_expand'])


SSD_STEP_BATCH = 8


def _ssd_sample_state_kernel(st_ref, xdt_t_ref, dec_t_ref, bc_ref, st_out, y_t_ref, *, bb):
    i0 = pl.program_id(0) * bb
    nb = xdt_t_ref.shape[1]
    lane = lax.broadcasted_iota(jnp.int32, (D_SSM, nb), 1)
    rows_per_group = D_SSM // N_GROUPS_S
    heads_per_group = N_HEADS_S // N_GROUPS_S

    @pl.when(i0 == 0)
    def _():
        y_t_ref[...] = jnp.zeros(y_t_ref.shape, F32)

    y_t = y_t_ref[...]
    for i in range(bb):
        b = i0 + i
        sel = lane == b
        xcol = jnp.sum(jnp.where(sel, xdt_t_ref[...], 0.0), axis=1, keepdims=True)
        dcol = jnp.sum(jnp.where(sel, dec_t_ref[...], 0.0), axis=1, keepdims=True)
        bc = bc_ref[pl.ds(b, 1), :]
        hs = st_ref[i].reshape(D_SSM, D_STATE)
        ycols = []
        for g in range(N_GROUPS_S):
            sl = slice(g * rows_per_group, (g + 1) * rows_per_group)
            brow = bc[:, g * D_STATE:(g + 1) * D_STATE]
            crow = bc[:, (N_GROUPS_S + g) * D_STATE:(N_GROUPS_S + g + 1) * D_STATE]
            new = hs[sl] * dcol[sl] + xcol[sl] * brow
            st_out[i, g * heads_per_group:(g + 1) * heads_per_group] = new.reshape(
                heads_per_group, SSM_HEAD_DIM, D_STATE)
            ycols.append(jnp.sum(new * crow, axis=1, keepdims=True))
        y_t = jnp.where(sel, jnp.concatenate(ycols, axis=0), y_t)
    y_t_ref[...] = y_t


def ssd_sample_state(li, state, xdt_t, dec_t, bc):
    nb = state.shape[1]
    bb = _tile(nb, SSD_STEP_BATCH)
    const = lambda b: (0, 0)
    st_spec = pl.BlockSpec((bb, N_HEADS_S, SSM_HEAD_DIM, D_STATE), lambda b: (b, 0, 0, 0))
    st_in_spec = pl.BlockSpec((None, bb, N_HEADS_S, SSM_HEAD_DIM, D_STATE), lambda b: (li, b, 0, 0, 0))
    return pl.pallas_call(
        functools.partial(_ssd_sample_state_kernel, bb=bb),
        out_shape=(jax.ShapeDtypeStruct(state.shape[1:], F32), jax.ShapeDtypeStruct((D_SSM, nb), F32)),
        grid=(nb // bb,),
        in_specs=[st_in_spec, pl.BlockSpec((D_SSM, nb), const), pl.BlockSpec((D_SSM, nb), const),
                  pl.BlockSpec((nb, 2 * N_GROUPS_S * D_STATE), const)],
        out_specs=(st_spec, pl.BlockSpec((D_SSM, nb), const)),
        compiler_params=_cparams(("arbitrary",)),
        name="ssd_sample_state",
    )(state, xdt_t, dec_t, bc)


def _sample_mix_kernel(y_t_ref, x_ref, z_ref, u_ref, v_ref, dsk_ref, gs_ref, gv_ref, wv_ref, bv_ref,
                       ssm_ref, cm_ref, vn_ref):
    yy = y_t_ref[...].T + x_ref[...] * dsk_ref[...]
    ssm_ref[...] = _rms(yy * _silu(z_ref[...]), gs_ref[...]).astype(BF16)
    vn = _rms(v_ref[...], gv_ref[...])
    vn_ref[...] = vn
    cm_ref[...] = (u_ref[...] * (vn * wv_ref[...] + bv_ref[...])).astype(BF16)


def sample_mix(y_t, x, proj, lw):
    nb = x.shape[0]
    const = lambda i: (0, 0)
    blk = lambda col: pl.BlockSpec((nb, D_SSM), lambda i, col=col: (0, col // D_SSM))
    vec = pl.BlockSpec((1, D_SSM), const)
    return pl.pallas_call(
        _sample_mix_kernel,
        out_shape=(jax.ShapeDtypeStruct((nb, D_SSM), BF16), jax.ShapeDtypeStruct((nb, D_CM), BF16),
                   jax.ShapeDtypeStruct((nb, D_CM), F32)),
        grid=(1,),
        in_specs=[pl.BlockSpec((D_SSM, nb), const), pl.BlockSpec((nb, D_SSM), const),
                  blk(COL_Z), blk(COL_U), blk(COL_V), vec, vec, vec, vec, vec],
        out_specs=(pl.BlockSpec((nb, D_SSM), const), pl.BlockSpec((nb, D_CM), const),
                   pl.BlockSpec((nb, D_CM), const)),
        compiler_params=_cparams(("arbitrary",)),
        name="sample_mix",
    )(y_t, x, proj, proj, proj, lw['d_skip'], lw['g_ssm_out'], lw['g_sgu'], lw['sgu_w0'], lw['sgu_b0'])


def _spatial_gate_kernel(u_ref, v_ref, g_ref, w_ref, bt_ref, o_ref, *, n_chunks):
    vn = _rms(v_ref[...], g_ref[...]).astype(BF16)
    ii = lax.broadcasted_iota(jnp.int32, (CHUNK, CHUNK), 0)
    jj = lax.broadcasted_iota(jnp.int32, (CHUNK, CHUNK), 1)
    bt = bt_ref[...]
    for g in range(N_CG):
        w = jnp.where(jj <= ii, w_ref[g], 0.0).astype(BF16)
        bcol = bt[:, g:g + 1]
        for c in range(n_chunks):
            rs = slice(c * CHUNK, (c + 1) * CHUNK)
            cs = slice(g * CG_DIM, (g + 1) * CG_DIM)
            mixed = _dot(w, vn[rs, cs]) + bcol
            o_ref[rs, cs] = (u_ref[rs, cs] * mixed).astype(BF16)


def spatial_gate(proj, lw, *, tm):
    r = proj.shape[0]
    const = lambda i: (0, 0)
    return pl.pallas_call(
        functools.partial(_spatial_gate_kernel, n_chunks=tm // CHUNK),
        out_shape=jax.ShapeDtypeStruct((r, D_CM), BF16),
        grid=(r // tm,),
        in_specs=[pl.BlockSpec((tm, D_CM), lambda i: (i, COL_U // D_CM)),
                  pl.BlockSpec((tm, D_CM), lambda i: (i, COL_V // D_CM)),
                  pl.BlockSpec((1, D_CM), const),
                  pl.BlockSpec((N_CG, CHUNK, CHUNK), lambda i: (0, 0, 0)),
                  pl.BlockSpec((CHUNK, N_CG), const)],
        out_specs=pl.BlockSpec((tm, D_CM), lambda i: (i, 0)),
        compiler_params=_cparams(("parallel",)),
        name="spatial_gate",
    )(proj, proj, lw['g_sgu'], lw['w_spatial'], lw['b_spatial_t'])


def _out_proj_kernel(x_ref, att_ref, ssm_ref, cm_ref, w_ref, o_ref):
    na = N_HEADS_A * V_DIM
    acc = _dot(att_ref[...], w_ref[0:na, :])
    acc = acc + _dot(ssm_ref[...], w_ref[na:na + D_SSM, :])
    acc = acc + _dot(cm_ref[...], w_ref[na + D_SSM:, :])
    o_ref[...] = x_ref[...] + acc


def out_proj(x, att, ssm, cm, w, *, tm, tn):
    r, d = x.shape
    w_mode = dict(pipeline_mode=pl.Buffered(1)) if tn == d else {}
    return pl.pallas_call(
        _out_proj_kernel,
        out_shape=jax.ShapeDtypeStruct((r, d), F32),
        grid=(r // tm, d // tn),
        in_specs=[pl.BlockSpec((tm, tn), lambda i, j: (i, j)),
                  pl.BlockSpec((tm, att.shape[1]), lambda i, j: (i, 0)),
                  pl.BlockSpec((tm, ssm.shape[1]), lambda i, j: (i, 0)),
                  pl.BlockSpec((tm, cm.shape[1]), lambda i, j: (i, 0)),
                  pl.BlockSpec((w.shape[0], tn), lambda i, j: (0, j), **w_mode)],
        out_specs=pl.BlockSpec((tm, tn), lambda i, j: (i, j)),
        compiler_params=_cparams(("parallel", "parallel")),
        name="out_proj",
    )(x, att, ssm, cm, w)


FFN_HALO = 16


def _ffn_prompt_kernel(x_ref, xp_ref, g_ref, wg_ref, wu_ref, cw_ref, cb_ref, wd_ref, o_ref, h_ref, gbuf, acc_ref,
                       *, tm, seq):
    i = pl.program_id(0)
    j = pl.program_id(1)

    @pl.when(j == 0)
    def _():
        keep = ((i * tm) % seq != 0).astype(F32)
        h_ref[0:FFN_HALO, :] = (_rms(xp_ref[...], g_ref[...]) * keep).astype(BF16)
        h_ref[FFN_HALO:, :] = _rms(x_ref[...], g_ref[...]).astype(BF16)
        acc_ref[...] = jnp.zeros(acc_ref.shape, F32)

    gbuf[...] = _dot(h_ref[...], wg_ref[...])
    up = _dot(h_ref[FFN_HALO:, :], wu_ref[...])
    cw = cw_ref[...]
    y = cb_ref[...]
    for t in range(FFN_CONV):
        off = FFN_HALO - (FFN_CONV - 1) + t
        y = y + cw[t:t + 1, :] * gbuf[off:off + tm, :]
    acc_ref[...] += _dot((_silu(y) * up).astype(BF16), wd_ref[...])

    @pl.when(j == pl.num_programs(1) - 1)
    def _():
        o_ref[...] = x_ref[...] + acc_ref[...]


def ffn_prompt(x, seq, lw, *, tm, tf):
    r, d = x.shape
    const = lambda i, j: (0, 0)
    halo_blocks = tm // FFN_HALO
    return pl.pallas_call(
        functools.partial(_ffn_prompt_kernel, tm=tm, seq=seq),
        out_shape=jax.ShapeDtypeStruct((r, d), F32),
        grid=(r // tm, D_FF // tf),
        in_specs=[pl.BlockSpec((tm, d), lambda i, j: (i, 0)),
                  pl.BlockSpec((FFN_HALO, d), lambda i, j: (jnp.maximum(i * halo_blocks - 1, 0), 0)),
                  pl.BlockSpec((1, d), const),
                  pl.BlockSpec((d, tf), lambda i, j: (0, j)),
                  pl.BlockSpec((d, tf), lambda i, j: (0, j)),
                  pl.BlockSpec((FFN_CONV, tf), lambda i, j: (0, j)),
                  pl.BlockSpec((1, tf), lambda i, j: (0, j)),
                  pl.BlockSpec((tf, d), lambda i, j: (j, 0))],
        out_specs=pl.BlockSpec((tm, d), lambda i, j: (i, 0)),
        scratch_shapes=[pltpu.VMEM((tm + FFN_HALO, d), BF16), pltpu.VMEM((tm + FFN_HALO, tf), F32),
                        pltpu.VMEM((tm, d), F32)],
        compiler_params=_cparams(("parallel", "arbitrary")),
        name="ffn_prompt",
    )(x, x, lw['g_ffn'], lw['w_gate'], lw['w_up'], lw['ffn_conv_w'], lw['ffn_conv_b'], lw['w_down'])


def _ffn_sample_kernel(x_ref, p0_ref, p1_ref, g_ref, wg_ref, wu_ref, cw_ref, cb_ref, wd_ref, o_ref, gate_ref,
                       h_ref, acc_ref):
    j = pl.program_id(0)

    @pl.when(j == 0)
    def _():
        h_ref[...] = _rms(x_ref[...], g_ref[...]).astype(BF16)
        acc_ref[...] = jnp.zeros(acc_ref.shape, F32)

    gate = _dot(h_ref[...], wg_ref[...])
    up = _dot(h_ref[...], wu_ref[...])
    gate_ref[...] = gate
    cw = cw_ref[...]
    y = cb_ref[...] + cw[0:1] * p0_ref[...] + cw[1:2] * p1_ref[...] + cw[2:3] * gate
    acc_ref[...] += _dot((_silu(y) * up).astype(BF16), wd_ref[...])

    @pl.when(j == pl.num_programs(0) - 1)
    def _():
        o_ref[...] = x_ref[...] + acc_ref[...]


def ffn_sample(x, prev, lw, *, tf):
    nb, d = x.shape
    nf = D_FF // tf
    return pl.pallas_call(
        _ffn_sample_kernel,
        out_shape=(jax.ShapeDtypeStruct((nb, d), F32), jax.ShapeDtypeStruct((nb, D_FF), F32)),
        grid=(nf,),
        in_specs=[pl.BlockSpec((nb, d), lambda j: (0, 0)),
                  pl.BlockSpec((nb, tf), lambda j: (0, j)),
                  pl.BlockSpec((nb, tf), lambda j: (0, nf + j)),
                  pl.BlockSpec((1, d), lambda j: (0, 0)),
                  pl.BlockSpec((d, tf), lambda j: (0, j)),
                  pl.BlockSpec((d, tf), lambda j: (0, j)),
                  pl.BlockSpec((FFN_CONV, tf), lambda j: (0, j)),
                  pl.BlockSpec((1, tf), lambda j: (0, j)),
                  pl.BlockSpec((tf, d), lambda j: (j, 0))],
        out_specs=(pl.BlockSpec((nb, d), lambda j: (0, 0)), pl.BlockSpec((nb, tf), lambda j: (0, j))),
        scratch_shapes=[pltpu.VMEM((nb, d), BF16), pltpu.VMEM((nb, d), F32)],
        compiler_params=_cparams(("arbitrary",)),
        name="ffn_sample",
    )(x, prev, prev, lw['g_ffn'], lw['w_gate'], lw['w_up'], lw['ffn_conv_w'], lw['ffn_conv_b'], lw['w_down'])


def _rope_tables(pos):
    inv = ROPE_THETA ** (-jnp.arange(0, ROPE_DIM, 2, dtype=F32) / ROPE_DIM)
    ang = pos.astype(F32)[:, None] * inv[None, :]
    cos, sin = jnp.cos(ang), jnp.sin(ang)
    zero = jnp.zeros_like(cos)
    return (jnp.concatenate([cos, cos, zero, zero], axis=1),
            jnp.concatenate([-sin, zero, zero, zero], axis=1),
            jnp.concatenate([zero, sin, zero, zero], axis=1))


def _pack_layer(li, p):
    d = D_MODEL
    w_in = p['w_in'][li]
    offs = np.cumsum([0, Q_LORA, KV_LORA, ROPE_DIM, D_SSM, CONV_DIM, N_HEADS_S, D_CM, D_CM])
    seg = lambda k: w_in[:, int(offs[k]):int(offs[k + 1])]
    zpad = lambda n: jnp.zeros((d, n), w_in.dtype)
    w_in_p = jnp.concatenate([seg(0), seg(3), seg(6), seg(7), seg(4), seg(1),
                              seg(2), zpad(LANE - ROPE_DIM), seg(5), zpad(LANE - N_HEADS_S)], axis=1).astype(BF16)
    h = N_HEADS_A
    w_uq = jnp.pad(p['w_uq'][li], ((0, 0), (0, 0), (0, QK_PAD - QK_DIM))).reshape(Q_LORA, h * QK_PAD).astype(BF16)
    w_uk = p['w_uk'][li].reshape(KV_LORA, h * NOPE_DIM).astype(BF16)
    w_uv = p['w_uv'][li].reshape(KV_LORA, h * V_DIM).astype(BF16)
    pad_g = lambda g: jnp.pad(g, (0, QK_PAD - QK_DIM))[None, :]
    row = lambda v: v[None, :]
    lane_pad = lambda v: jnp.pad(v, (0, LANE - v.shape[0]))[None, :]
    head_expand = np.zeros((LANE, D_SSM), np.float32)
    head_expand[:N_HEADS_S] = np.kron(np.eye(N_HEADS_S, dtype=np.float32), np.ones((1, SSM_HEAD_DIM), np.float32))
    head_expand = jnp.asarray(head_expand)
    w_gu = p['w_gate_up'][li]
    return {
        'g_mix': row(p['g_mix'][li]), 'w_in': w_in_p,
        'g_q_a': row(p['g_q_a'][li]), 'g_kv_a': row(p['g_kv_a'][li]),
        'w_uq': w_uq, 'w_uk': w_uk, 'w_uk_t': w_uk.T, 'w_uv': w_uv,
        'gq': pad_g(p['g_qk_q'][li]) * (QK_DIM ** -0.5),
        'gk': pad_g(p['g_qk_k'][li]),
        'ssm_conv_w': p['ssm_conv_w'][li], 'ssm_conv_b': row(p['ssm_conv_b'][li]),
        'dt_bias': lane_pad(p['ssm_dt_bias'][li]), 'a_log': lane_pad(p['ssm_a_log'][li]),
        'd_skip': row(jnp.repeat(p['ssm_d'][li], SSM_HEAD_DIM)),
        'g_ssm_out': row(p['g_ssm_out'][li]), 'head_expand': head_expand,
        'g_sgu': row(p['g_sgu'][li]), 'w_spatial': p['w_spatial'][li], 'b_spatial_t': p['b_spatial'][li].T,
        'sgu_w0': row(jnp.repeat(p['w_spatial'][li][:, 0, 0], CG_DIM)),
        'sgu_b0': row(jnp.repeat(p['b_spatial'][li][:, 0], CG_DIM)),
        'w_out': p['w_out'][li].astype(BF16),
        'g_ffn': row(p['g_ffn'][li]),
        'w_gate': w_gu[:, :D_FF].astype(BF16), 'w_up': w_gu[:, D_FF:].astype(BF16),
        'ffn_conv_w': p['ffn_conv_w'][li], 'ffn_conv_b': row(p['ffn_conv_b'][li]),
        'w_down': p['w_down'][li].astype(BF16),
    }


def _tile(n, pref):
    t = min(n, pref)
    assert n % t == 0, (n, pref)
    return t


def kernel(x_prompt, x_sample, cache_latent, cache_krope, state_ssm, state_conv, state_ffn_conv, page_table, g_mix, w_in, g_q_a, w_uq, g_kv_a, w_uk, w_uv, g_qk_q, g_qk_k, ssm_conv_w, ssm_conv_b, ssm_dt_bias, ssm_a_log, ssm_d, g_ssm_out, g_sgu, w_spatial, b_spatial, w_out, g_ffn, w_gate_up, ffn_conv_w, ffn_conv_b, w_down):
    params = dict(g_mix=g_mix, w_in=w_in, g_q_a=g_q_a, w_uq=w_uq, g_kv_a=g_kv_a, w_uk=w_uk, w_uv=w_uv,
                  g_qk_q=g_qk_q, g_qk_k=g_qk_k, ssm_conv_w=ssm_conv_w, ssm_conv_b=ssm_conv_b,
                  ssm_dt_bias=ssm_dt_bias, ssm_a_log=ssm_a_log, ssm_d=ssm_d, g_ssm_out=g_ssm_out, g_sgu=g_sgu,
                  w_spatial=w_spatial, b_spatial=b_spatial, w_out=w_out, g_ffn=g_ffn, w_gate_up=w_gate_up,
                  ffn_conv_w=ffn_conv_w, ffn_conv_b=ffn_conv_b, w_down=w_down)
    depth = w_in.shape[0]
    bp, seq, d = x_prompt.shape
    bd, dec_seq, _ = x_sample.shape
    assert dec_seq == 1 and d == D_MODEL and cache_latent.shape[2] == PAGE
    n_pages = page_table.shape[1]
    past_len = n_pages * PAGE
    rp = bp * seq

    tabs_p = _rope_tables(jnp.arange(seq, dtype=jnp.int32))
    tabs_s = _rope_tables(jnp.full((bd,), past_len, jnp.int32))

    tm_p = _tile(seq, 512)
    tq = _tile(seq, 1024)
    tk = tq
    pp = next(c for c in (16, 8, 4, 2) if n_pages % c == 0)
    cache_krope_t = jnp.swapaxes(cache_krope, 2, 3)
    tf = 512

    yp = x_prompt.reshape(rp, d)
    ys = x_sample.reshape(bd, d)
    outs = [[] for _ in range(11)]
    for li in range(depth):
        lw = _pack_layer(li, params)

        proj = norm_matmul(yp, lw['g_mix'], lw['w_in'], tm=tm_p, tn=IN_PAD)
        q, k, v, lat, kro = mla_prep(proj, bp, seq, lw, lw['gq'] * LOG2E, tabs_p, tm=_tile(seq, 256), q_dtype=BF16)
        att = flash_attention(q, k, v, tq=tq, tk=tk)
        ssm, ssm_state = ssd_prompt(proj, bp, seq, lw)
        cm = spatial_gate(proj, lw, tm=tm_p)
        y_mid = out_proj(yp, att.reshape(rp, -1), ssm, cm, lw['w_out'], tm=tm_p, tn=d)
        yp = ffn_prompt(y_mid, seq, lw, tm=tm_p, tf=tf)
        tail = y_mid.reshape(bp, seq, d)[:, seq - (FFN_CONV - 1):, :].reshape(bp * (FFN_CONV - 1), d)
        tail = jnp.pad(tail, ((0, 8 - tail.shape[0]), (0, 0)))
        gate_tail = norm_matmul(tail, lw['g_ffn'], lw['w_gate'], tm=8, tn=tf)[:bp * (FFN_CONV - 1)]
        proj3 = proj.reshape(bp, seq, IN_PAD)
        outs[0].append(lat)
        outs[1].append(kro)
        outs[2].append(ssm_state)
        outs[3].append(proj3[:, seq - (SSM_CONV - 1):, COL_XBC:COL_XBC + CONV_DIM])
        outs[4].append(gate_tail.reshape(bp, FFN_CONV - 1, D_FF))

        proj_s = norm_matmul(ys, lw['g_mix'], lw['w_in'], tm=bd, tn=IN_PAD)
        q_s, k_s, _, lat_s, kro_s = mla_prep(proj_s, 1, bd, lw, lw['gq'], tabs_s, tm=bd, q_dtype=F32)
        qabs, qgr = dec_q(q_s, lw['gk'], lw['w_uk'])
        m, l, acc = dec_paged(li, page_table, cache_latent, cache_krope_t, lw['w_uk_t'],
                              qabs.transpose(1, 0, 2), qgr.transpose(1, 0, 2), pp=pp)
        att_s = dec_finish(m.transpose(1, 0, 2), l.transpose(1, 0, 2), acc.transpose(1, 0, 2), q_s, k_s,
                           lat_s[0], lw['w_uv'])
        conv_prev = state_conv[li]
        x_s, bc_s, xdt_t, dec_t = ssd_sample_prep(proj_s, conv_prev.reshape(bd, -1), lw)
        ssm_state_s, y_t = ssd_sample_state(li, state_ssm, xdt_t, dec_t, bc_s)
        ssm_s, cm_s, vn_s = sample_mix(y_t, x_s, proj_s, lw)
        ys_mid = out_proj(ys, att_s, ssm_s, cm_s, lw['w_out'], tm=bd, tn=d)
        ffn_prev = state_ffn_conv[li]
        ys, gate_s = ffn_sample(ys_mid, ffn_prev.reshape(bd, -1), lw, tf=tf)
        outs[5].append(lat_s.reshape(bd, 1, KV_LORA))
        outs[6].append(kro_s.reshape(bd, 1, ROPE_DIM))
        outs[7].append(ssm_state_s)
        outs[8].append(jnp.concatenate([conv_prev[:, 1:], proj_s[:, None, COL_XBC:COL_XBC + CONV_DIM]], axis=1))
        outs[9].append(jnp.concatenate([ffn_prev[:, 1:], gate_s[:, None, :]], axis=1))
        outs[10].append(vn_s.reshape(bd, 1, D_CM))

    stacked = [jnp.stack(o) for o in outs]
    return (yp.reshape(bp, seq, d), ys.reshape(bd, 1, d), *stacked)
```

```python
import functools

import numpy as np
import jax
import jax.numpy as jnp
from jax import lax
from jax.experimental import pallas as pl
from jax.experimental.pallas import tpu as pltpu

F32 = jnp.float32
BF16 = jnp.bfloat16
EPS = 1e-6
HIGHEST = lax.Precision.HIGHEST

D_MODEL = 2048
N_HEADS_A = 8
NOPE_DIM = 128
ROPE_DIM = 64
QK_DIM = NOPE_DIM + ROPE_DIM
QK_PAD = 256
V_DIM = 128
Q_LORA = 512
KV_LORA = 256
ROPE_THETA = 10000.0
D_SSM = 512
SSM_HEAD_DIM = 64
N_HEADS_S = 8
N_GROUPS_S = 2
D_STATE = 128
SSM_CONV = 4
SSD_CHUNK = 128
CONV_DIM = 1024
D_CM = 512
CHUNK = 128
N_CG = 4
CG_DIM = 128
D_FF = 5632
FFN_CONV = 3
PAGE = 128

COL_CQ, COL_Z, COL_U, COL_V, COL_XBC, COL_CKV, COL_KR, COL_DT = 0, 512, 1024, 1536, 2048, 3072, 3328, 3456
IN_PAD = 3584

LANE = 128
VMEM_LIMIT = 56 * 1024 * 1024

NT_DIMS = (((1,), (1,)), ((), ()))


def _cparams(sem):
    return pltpu.CompilerParams(dimension_semantics=sem, vmem_limit_bytes=VMEM_LIMIT)


def _rms(x, g):
    return x * lax.rsqrt(jnp.mean(x * x, axis=-1, keepdims=True) + EPS) * g


def _silu(x):
    return x * (1.0 / (1.0 + jnp.exp(-x)))


def _softplus(x):
    return jnp.maximum(x, 0.0) + jnp.log(1.0 + jnp.exp(-jnp.abs(x)))


def _dot(a, b):
    return jnp.dot(a, b, preferred_element_type=F32)


def _dot_nt(a, b):
    return lax.dot_general(a, b, NT_DIMS, preferred_element_type=F32)


def _norm_matmul_kernel(x_ref, g_ref, w_ref, o_ref, h_ref):
    @pl.when(pl.program_id(1) == 0)
    def _():
        h_ref[...] = _rms(x_ref[...], g_ref[...]).astype(BF16)

    o_ref[...] = _dot(h_ref[...], w_ref[...])


def norm_matmul(x, g, w, *, tm, tn, n=None):
    r, k = x.shape
    n = w.shape[1] if n is None else n
    w_mode = dict(pipeline_mode=pl.Buffered(1)) if tn == n else {}
    return pl.pallas_call(
        _norm_matmul_kernel,
        out_shape=jax.ShapeDtypeStruct((r, n), F32),
        grid=(r // tm, n // tn),
        in_specs=[pl.BlockSpec((tm, k), lambda i, j: (i, 0)),
                  pl.BlockSpec((1, k), lambda i, j: (0, 0)),
                  pl.BlockSpec((k, tn), lambda i, j: (0, j), **w_mode)],
        out_specs=pl.BlockSpec((tm, tn), lambda i, j: (i, j)),
        scratch_shapes=[pltpu.VMEM((tm, k), BF16)],
        compiler_params=_cparams(("parallel", "arbitrary")),
        name="norm_matmul",
    )(x, g, w)


def _mla_prep_kernel(cq_ref, ckv_ref, kr_ref, gqa_ref, gkva_ref, wuq_ref, wuk_ref, wuv_ref, gq_ref, gk_ref,
                     cos_ref, s1_ref, s2_ref, q_ref, k_ref, v_ref, lat_ref, kro_ref):
    cos = cos_ref[...]
    s1 = s1_ref[...]
    s2 = s2_ref[...]

    def rope(b):
        return b * cos + pltpu.roll(b, 96, 1) * s1 + pltpu.roll(b, 32, 1) * s2

    def head_norm(a, b, g):
        ss = jnp.sum(a * a + b * b, axis=-1, keepdims=True)
        inv = lax.rsqrt(ss * (1.0 / QK_DIM) + EPS)
        return a * inv * g[:, :NOPE_DIM], b * inv * g[:, NOPE_DIM:]

    cqn = _rms(cq_ref[...], gqa_ref[...]).astype(BF16)
    qf = _dot(cqn, wuq_ref[...])
    gq = gq_ref[...]
    for h in range(N_HEADS_A):
        a = qf[:, h * QK_PAD:h * QK_PAD + NOPE_DIM]
        b = rope(qf[:, h * QK_PAD + NOPE_DIM:(h + 1) * QK_PAD])
        a, b = head_norm(a, b, gq)
        q_ref[0, h, :, 0:NOPE_DIM] = a.astype(q_ref.dtype)
        q_ref[0, h, :, NOPE_DIM:QK_PAD] = b.astype(q_ref.dtype)

    ckvn = _rms(ckv_ref[...], gkva_ref[...])
    lat_ref[0] = ckvn
    cb = ckvn.astype(BF16)
    kn = _dot(cb, wuk_ref[...])
    vf = _dot(cb, wuv_ref[...])
    krr = rope(kr_ref[...])
    kro_ref[0] = krr[:, :ROPE_DIM]
    gk = gk_ref[...]
    for h in range(N_HEADS_A):
        a, b = head_norm(kn[:, h * NOPE_DIM:(h + 1) * NOPE_DIM], krr, gk)
        k_ref[0, h, :, 0:NOPE_DIM] = a.astype(BF16)
        k_ref[0, h, :, NOPE_DIM:QK_PAD] = b.astype(BF16)
        v_ref[0, h] = vf[:, h * V_DIM:(h + 1) * V_DIM].astype(BF16)


def mla_prep(proj, nb, seq, lw, gq, tabs, *, tm, q_dtype):
    nt = seq // tm
    row = lambda b, t: (b * nt + t, 0)
    const = lambda b, t: (0, 0)
    tab = lambda b, t: (t, 0)
    h = N_HEADS_A
    return pl.pallas_call(
        _mla_prep_kernel,
        out_shape=(jax.ShapeDtypeStruct((nb, h, seq, QK_PAD), q_dtype),
                   jax.ShapeDtypeStruct((nb, h, seq, QK_PAD), BF16),
                   jax.ShapeDtypeStruct((nb, h, seq, V_DIM), BF16),
                   jax.ShapeDtypeStruct((nb, seq, KV_LORA), F32),
                   jax.ShapeDtypeStruct((nb, seq, ROPE_DIM), F32)),
        grid=(nb, nt),
        in_specs=[pl.BlockSpec((tm, Q_LORA), lambda b, t: (b * nt + t, COL_CQ // Q_LORA)),
                  pl.BlockSpec((tm, KV_LORA), lambda b, t: (b * nt + t, COL_CKV // KV_LORA)),
                  pl.BlockSpec((tm, LANE), lambda b, t: (b * nt + t, COL_KR // LANE)),
                  pl.BlockSpec((1, Q_LORA), const),
                  pl.BlockSpec((1, KV_LORA), const),
                  pl.BlockSpec((Q_LORA, h * QK_PAD), const),
                  pl.BlockSpec((KV_LORA, h * NOPE_DIM), const),
                  pl.BlockSpec((KV_LORA, h * V_DIM), const),
                  pl.BlockSpec((1, QK_PAD), const),
                  pl.BlockSpec((1, QK_PAD), const),
                  pl.BlockSpec((tm, LANE), tab),
                  pl.BlockSpec((tm, LANE), tab),
                  pl.BlockSpec((tm, LANE), tab)],
        out_specs=(pl.BlockSpec((1, h, tm, QK_PAD), lambda b, t: (b, 0, t, 0)),
                   pl.BlockSpec((1, h, tm, QK_PAD), lambda b, t: (b, 0, t, 0)),
                   pl.BlockSpec((1, h, tm, V_DIM), lambda b, t: (b, 0, t, 0)),
                   pl.BlockSpec((1, tm, KV_LORA), lambda b, t: (b, t, 0)),
                   pl.BlockSpec((1, tm, ROPE_DIM), lambda b, t: (b, t, 0))),
        compiler_params=_cparams(("parallel", "parallel")),
        name="mla_prep",
    )(proj, proj, proj, lw['g_q_a'], lw['g_kv_a'], lw['w_uq'], lw['w_uk'], lw['w_uv'], gq, lw['gk'],
      tabs[0], tabs[1], tabs[2])


FLASH_ROWS = 256
LOG2E = 1.4426950408889634


def _flash_kernel(qi_ref, ki_ref, q_ref, k_ref, v_ref, o_ref, m_ref, l_ref, acc_ref, *, tq, tk):
    p = pl.program_id(2)
    qi = qi_ref[p]
    ki = ki_ref[p]

    @pl.when(ki == 0)
    def _():
        m_ref[...] = jnp.full(m_ref.shape, -jnp.inf, F32)
        l_ref[...] = jnp.zeros(l_ref.shape, F32)
        acc_ref[...] = jnp.zeros(acc_ref.shape, F32)

    n_sub = tq // FLASH_ROWS

    def scores(r, n_cols):
        rows = slice(r * FLASH_ROWS, (r + 1) * FLASH_ROWS)
        return _dot_nt(q_ref[0, 0, rows, :], k_ref[0, 0, 0:n_cols, :])

    def update(r, s, masked):
        rows = slice(r * FLASH_ROWS, (r + 1) * FLASH_ROWS)
        n_cols = s.shape[1]
        if masked:
            row = r * FLASH_ROWS + lax.broadcasted_iota(jnp.int32, s.shape, 0)
            col = lax.broadcasted_iota(jnp.int32, s.shape, 1)
            s = jnp.where(col <= row, s, -jnp.inf)
        m_prev = m_ref[rows, :]
        m_new = jnp.maximum(m_prev, jnp.max(s, axis=-1, keepdims=True))
        alpha = jnp.exp2(m_prev - m_new)
        pr = jnp.exp2(s - m_new)
        l_ref[rows, :] = alpha * l_ref[rows, :] + jnp.sum(pr, axis=-1, keepdims=True)
        acc_ref[rows, :] = alpha * acc_ref[rows, :] + _dot(pr.astype(BF16), v_ref[0, 0, 0:n_cols, :])
        m_ref[rows, :] = m_new

    def sweep(n_cols_of, masked):
        s_next = scores(0, n_cols_of(0))
        for r in range(n_sub):
            s = s_next
            if r + 1 < n_sub:
                s_next = scores(r + 1, n_cols_of(r + 1))
            update(r, s, masked)

    @pl.when(ki < qi)
    def _():
        sweep(lambda r: tk, False)

    @pl.when(ki == qi)
    def _():
        sweep(lambda r: (r + 1) * FLASH_ROWS, True)
        o_ref[0] = (acc_ref[...] / l_ref[...]).astype(BF16)


def flash_attention(q, k, v, *, tq, tk):
    b, h, s, _ = q.shape
    assert tq == tk and tq % FLASH_ROWS == 0
    pairs = [(i, j) for i in range(s // tq) for j in range(i + 1)]
    qi_tab = jnp.asarray(np.array([p[0] for p in pairs], np.int32))
    ki_tab = jnp.asarray(np.array([p[1] for p in pairs], np.int32))
    return pl.pallas_call(
        functools.partial(_flash_kernel, tq=tq, tk=tk),
        out_shape=jax.ShapeDtypeStruct((b, s, h * V_DIM), BF16),
        grid_spec=pltpu.PrefetchScalarGridSpec(
            num_scalar_prefetch=2,
            grid=(b, h, len(pairs)),
            in_specs=[pl.BlockSpec((1, 1, tq, QK_PAD), lambda bi, hi, p, qt, kt: (bi, hi, qt[p], 0)),
                      pl.BlockSpec((1, 1, tk, QK_PAD), lambda bi, hi, p, qt, kt: (bi, hi, kt[p], 0)),
                      pl.BlockSpec((1, 1, tk, V_DIM), lambda bi, hi, p, qt, kt: (bi, hi, kt[p], 0))],
            out_specs=pl.BlockSpec((1, tq, V_DIM), lambda bi, hi, p, qt, kt: (bi, qt[p], hi)),
            scratch_shapes=[pltpu.VMEM((tq, 1), F32), pltpu.VMEM((tq, 1), F32), pltpu.VMEM((tq, V_DIM), F32)]),
        compiler_params=_cparams(("parallel", "parallel", "arbitrary")),
        name="flash_attention",
    )(qi_tab, ki_tab, q, k, v)


def _dec_q_kernel(q_ref, gk_ref, wuk_ref, qabs_ref, qgr_ref):
    gk = gk_ref[...]
    nb = q_ref.shape[2]
    for h in range(N_HEADS_A):
        qg = q_ref[0, h] * gk
        qn = qg[:, :NOPE_DIM].astype(BF16)
        qabs_ref[h] = _dot_nt(qn, wuk_ref[:, h * NOPE_DIM:(h + 1) * NOPE_DIM]).astype(BF16)
        qgr_ref[h] = qg[:, NOPE_DIM:QK_DIM].astype(BF16)
    for h in range(N_HEADS_A, 2 * N_HEADS_A):
        qabs_ref[h] = jnp.zeros((nb, KV_LORA), BF16)
        qgr_ref[h] = jnp.zeros((nb, ROPE_DIM), BF16)


def dec_q(q, gk, wuk):
    nb = q.shape[2]
    return pl.pallas_call(
        _dec_q_kernel,
        out_shape=(jax.ShapeDtypeStruct((2 * N_HEADS_A, nb, KV_LORA), BF16),
                   jax.ShapeDtypeStruct((2 * N_HEADS_A, nb, ROPE_DIM), BF16)),
        compiler_params=pltpu.CompilerParams(vmem_limit_bytes=VMEM_LIMIT),
        name="dec_q",
    )(q, gk, wuk)


def _dec_paged_kernel(pt_ref, lat_hbm, krt_hbm, wukt_ref, qabs_ref, qgr_ref, m_out, l_out, acc_out,
                      latbuf, krbuf, sem, lhs_ref, cbuf, m_s, l_s, acc_s, *, li, pp):
    j = pl.program_id(1)
    nj = pl.num_programs(1)
    step = pl.program_id(0) * nj + j
    last_step = pl.num_programs(0) * nj - 1
    slot = step % 2
    nh = N_HEADS_A
    n_k = nh * NOPE_DIM

    def page_copies(st, sl):
        row = st // nj
        col = (st % nj) * pp
        copies = []
        for i in range(pp):
            pid = pt_ref[row, col + i]
            copies.append(pltpu.make_async_copy(lat_hbm.at[li, pid], latbuf.at[sl, i], sem.at[0, sl]))
            copies.append(pltpu.make_async_copy(krt_hbm.at[li, pid], krbuf.at[sl, i], sem.at[1, sl]))
        return copies

    @pl.when(step == 0)
    def _():
        for c in page_copies(step, slot):
            c.start()

    @pl.when(step < last_step)
    def _():
        for c in page_copies(step + 1, 1 - slot):
            c.start()

    @pl.when(j == 0)
    def _():
        lhs_ref[0:n_k, :] = wukt_ref[...]
        lhs_ref[n_k:n_k + 2 * nh, :] = qabs_ref[0]
        m_s[...] = jnp.full(m_s.shape, -jnp.inf, F32)
        l_s[...] = jnp.zeros(l_s.shape, F32)
        acc_s[...] = jnp.zeros(acc_s.shape, F32)

    for c in page_copies(step, slot):
        c.wait()

    qgr = qgr_ref[0]
    parts = []
    for t in range(pp // 2):
        cb = jnp.concatenate([latbuf[slot, 2 * t], latbuf[slot, 2 * t + 1]], axis=0).astype(BF16)
        cbuf[t * 2 * PAGE:(t + 1) * 2 * PAGE, :] = cb
        kt = _dot_nt(lhs_ref[...], cb)
        k3 = kt[:n_k].reshape(nh, NOPE_DIM, 2 * PAGE)
        ssq = jnp.sum(k3 * k3, axis=1)
        krt = jnp.concatenate([krbuf[slot, 2 * t], krbuf[slot, 2 * t + 1]], axis=1)
        ssq = ssq + jnp.sum(krt * krt, axis=0, keepdims=True)
        num = kt[n_k:n_k + nh] + _dot(qgr, krt.astype(BF16))[:nh]
        parts.append(num * lax.rsqrt(ssq * (1.0 / QK_DIM) + EPS))
    s = jnp.concatenate(parts, axis=1)
    m_prev = m_s[...]
    m_new = jnp.maximum(m_prev, jnp.max(s, axis=-1, keepdims=True))
    alpha = jnp.exp(m_prev - m_new)
    p = jnp.exp(s - m_new)
    l_s[...] = alpha * l_s[...] + jnp.sum(p, axis=-1, keepdims=True)
    pb = jnp.concatenate([p, jnp.zeros_like(p)], axis=0).astype(BF16)
    acc_s[...] = alpha * acc_s[...] + _dot(pb, cbuf[...])[:nh]
    m_s[...] = m_new

    @pl.when(j == pl.num_programs(1) - 1)
    def _():
        m_out[0] = jnp.broadcast_to(m_s[...], (nh, LANE))
        l_out[0] = jnp.broadcast_to(l_s[...], (nh, LANE))
        acc_out[0] = acc_s[...]


def dec_paged(li, page_table, cache_lat, cache_kr_t, wukt, qabs, qgr, *, pp):
    nb, n_pages = page_table.shape
    nh = N_HEADS_A
    return pl.pallas_call(
        functools.partial(_dec_paged_kernel, li=li, pp=pp),
        out_shape=(jax.ShapeDtypeStruct((nb, nh, LANE), F32),
                   jax.ShapeDtypeStruct((nb, nh, LANE), F32),
                   jax.ShapeDtypeStruct((nb, nh, KV_LORA), F32)),
        grid_spec=pltpu.PrefetchScalarGridSpec(
            num_scalar_prefetch=1,
            grid=(nb, n_pages // pp),
            in_specs=[
                pl.BlockSpec(memory_space=pl.ANY),
                pl.BlockSpec(memory_space=pl.ANY),
                pl.BlockSpec((nh * NOPE_DIM, KV_LORA), lambda b, j, pt: (0, 0)),
                pl.BlockSpec((1, 2 * nh, KV_LORA), lambda b, j, pt: (b, 0, 0)),
                pl.BlockSpec((1, 2 * nh, ROPE_DIM), lambda b, j, pt: (b, 0, 0))],
            out_specs=(pl.BlockSpec((1, nh, LANE), lambda b, j, pt: (b, 0, 0)),
                       pl.BlockSpec((1, nh, LANE), lambda b, j, pt: (b, 0, 0)),
                       pl.BlockSpec((1, nh, KV_LORA), lambda b, j, pt: (b, 0, 0))),
            scratch_shapes=[pltpu.VMEM((2, pp, PAGE, KV_LORA), F32),
                            pltpu.VMEM((2, pp, ROPE_DIM, PAGE), F32),
                            pltpu.SemaphoreType.DMA((2, 2)),
                            pltpu.VMEM((nh * NOPE_DIM + 2 * nh, KV_LORA), BF16),
                            pltpu.VMEM((pp * PAGE, KV_LORA), BF16),
                            pltpu.VMEM((nh, 1), F32), pltpu.VMEM((nh, 1), F32), pltpu.VMEM((nh, KV_LORA), F32)]),
        compiler_params=_cparams(("arbitrary", "arbitrary")),
        name="dec_paged",
    )(page_table, cache_lat, cache_kr_t, wukt, qabs, qgr)


def _dec_finish_kernel(m_ref, l_ref, acc_ref, q_ref, k_ref, lat_ref, wuv_ref, o_ref):
    cn = lat_ref[...].astype(BF16).astype(F32)
    for h in range(N_HEADS_A):
        qh = q_ref[0, h].astype(BF16).astype(F32)
        s_new = jnp.sum(qh * k_ref[0, h].astype(F32), axis=-1, keepdims=True)
        m_h = m_ref[h][:, 0:1]
        l_h = l_ref[h][:, 0:1]
        m_tot = jnp.maximum(m_h, s_new)
        a = jnp.exp(m_h - m_tot)
        pn = jnp.exp(s_new - m_tot)
        lat = (acc_ref[h] * a + pn * cn) / (l_h * a + pn)
        o_ref[:, h * V_DIM:(h + 1) * V_DIM] = _dot(lat.astype(BF16), wuv_ref[:, h * V_DIM:(h + 1) * V_DIM]).astype(BF16)


def dec_finish(m, l, acc, q, k, lat, wuv):
    nb = lat.shape[0]
    return pl.pallas_call(
        _dec_finish_kernel,
        out_shape=jax.ShapeDtypeStruct((nb, N_HEADS_A * V_DIM), BF16),
        compiler_params=pltpu.CompilerParams(vmem_limit_bytes=VMEM_LIMIT),
        name="dec_finish",
    )(m, l, acc, q, k, lat, wuv)


def _ssd_prompt_kernel(xbc_ref, dt_ref, z_ref, cw_ref, cb_ref, dtb_ref, alog_ref, dsk_ref, g_ref, exp_ref,
                       y_ref, st_ref, xbuf, state):
    c = pl.program_id(1)
    lc = SSD_CHUNK
    hd = SSM_HEAD_DIM

    @pl.when(c == 0)
    def _():
        xbuf[0:8, :] = jnp.zeros((8, CONV_DIM), F32)
        state[...] = jnp.zeros(state.shape, F32)

    xbuf[8:8 + lc, :] = xbc_ref[...]
    cw = cw_ref[...]
    y = cb_ref[...]
    for i in range(SSM_CONV):
        y = y + cw[i:i + 1, :] * xbuf[8 - (SSM_CONV - 1) + i:8 - (SSM_CONV - 1) + i + lc, :]
    xbuf[0:8, :] = xbc_ref[lc - 8:lc, :]
    xc = _silu(y)
    x = xc[:, :D_SSM]
    dt = _softplus(dt_ref[...] + dtb_ref[...])
    da = dt * (-jnp.exp(alog_ref[...]))
    ii = lax.broadcasted_iota(jnp.int32, (lc, lc), 0)
    jj = lax.broadcasted_iota(jnp.int32, (lc, lc), 1)
    causal = jj <= ii
    acs = jnp.dot(causal.astype(F32), da, precision=HIGHEST, preferred_element_type=F32)
    acs_t = acs.T
    xdt = x * jnp.dot(dt, exp_ref[...], precision=HIGHEST, preferred_element_type=F32)
    xdt_t = xdt.T.astype(BF16)
    xdt_b = xdt.astype(BF16)
    ys = []
    for g in range(N_GROUPS_S):
        bm = xc[:, D_SSM + g * D_STATE:D_SSM + (g + 1) * D_STATE]
        cm = xc[:, D_SSM + (N_GROUPS_S + g) * D_STATE:D_SSM + (N_GROUPS_S + g + 1) * D_STATE]
        cmb = cm.astype(BF16)
        gmat = _dot_nt(cmb, bm.astype(BF16))
        for hh in range(N_HEADS_S // N_GROUPS_S):
            h = g * (N_HEADS_S // N_GROUPS_S) + hh
            a_col = acs[:, h:h + 1]
            a_row = acs_t[h:h + 1, :]
            a_last = acs[lc - 1:lc, h:h + 1]
            lmat = jnp.exp(jnp.where(causal, a_col - a_row, -jnp.inf))
            y_diag = _dot((gmat * lmat).astype(BF16), xdt_b[:, h * hd:(h + 1) * hd])
            st_prev = state[h]
            y_off = _dot_nt(cmb, st_prev.astype(BF16)) * jnp.exp(a_col)
            bdec = (bm * jnp.exp(a_last - a_col)).astype(BF16)
            state[h] = jnp.exp(a_last) * st_prev + _dot(xdt_t[h * hd:(h + 1) * hd, :], bdec)
            ys.append(y_diag + y_off)
    yy = jnp.concatenate(ys, axis=1) + x * dsk_ref[...]
    y_ref[...] = _rms(yy * _silu(z_ref[...]), g_ref[...]).astype(BF16)

    @pl.when(c == pl.num_programs(1) - 1)
    def _():
        st_ref[0] = state[...]


def ssd_prompt(proj, nb, seq, lw):
    nc = seq // SSD_CHUNK
    lc = SSD_CHUNK
    const = lambda b, c: (0, 0)
    return pl.pallas_call(
        _ssd_prompt_kernel,
        out_shape=(jax.ShapeDtypeStruct((nb * seq, D_SSM), BF16),
                   jax.ShapeDtypeStruct((nb, N_HEADS_S, SSM_HEAD_DIM, D_STATE), F32)),
        grid=(nb, nc),
        in_specs=[pl.BlockSpec((lc, CONV_DIM), lambda b, c: (b * nc + c, COL_XBC // CONV_DIM)),
                  pl.BlockSpec((lc, LANE), lambda b, c: (b * nc + c, COL_DT // LANE)),
                  pl.BlockSpec((lc, D_SSM), lambda b, c: (b * nc + c, COL_Z // D_SSM)),
                  pl.BlockSpec((SSM_CONV, CONV_DIM), const),
                  pl.BlockSpec((1, CONV_DIM), const),
                  pl.BlockSpec((1, LANE), const),
                  pl.BlockSpec((1, LANE), const),
                  pl.BlockSpec((1, D_SSM), const),
                  pl.BlockSpec((1, D_SSM), const),
                  pl.BlockSpec((LANE, D_SSM), const)],
        out_specs=(pl.BlockSpec((lc, D_SSM), lambda b, c: (b * nc + c, 0)),
                   pl.BlockSpec((1, N_HEADS_S, SSM_HEAD_DIM, D_STATE), lambda b, c: (b, 0, 0, 0))),
        scratch_shapes=[pltpu.VMEM((8 + lc, CONV_DIM), F32),
                        pltpu.VMEM((N_HEADS_S, SSM_HEAD_DIM, D_STATE), F32)],
        compiler_params=_cparams(("parallel", "arbitrary")),
        name="ssd_prompt",
    )(proj, proj, proj, lw['ssm_conv_w'], lw['ssm_conv_b'], lw['dt_bias'], lw['a_log'], lw['d_skip'],
      lw['g_ssm_out'], lw['head_expand'])


def _ssd_sample_prep_kernel(xbc_ref, p0_ref, p1_ref, p2_ref, dt_ref, cw_ref, cb_ref, dtb_ref, alog_ref, exp_ref,
                            x_ref, bc_ref, xdt_t_ref, dec_t_ref):
    cw = cw_ref[...]
    y = (cb_ref[...] + cw[0:1] * p0_ref[...] + cw[1:2] * p1_ref[...] + cw[2:3] * p2_ref[...]
         + cw[3:4] * xbc_ref[...])
    xc = _silu(y)
    x = xc[:, :D_SSM]
    dt = _softplus(dt_ref[...] + dtb_ref[...])
    dec = jnp.exp(dt * (-jnp.exp(alog_ref[...])))
    e = exp_ref[...]
    x_ref[...] = x
    bc_ref[...] = xc[:, D_SSM:]
    xdt_t_ref[...] = (x * jnp.dot(dt, e, precision=HIGHEST, preferred_element_type=F32)).T
    dec_t_ref[...] = jnp.dot(dec, e, precision=HIGHEST, preferred_element_type=F32).T


def ssd_sample_prep(proj, conv_state, lw):
    nb = proj.shape[0]
    const = lambda i: (0, 0)
    return pl.pallas_call(
        _ssd_sample_prep_kernel,
        out_shape=(jax.ShapeDtypeStruct((nb, D_SSM), F32),
                   jax.ShapeDtypeStruct((nb, 2 * N_GROUPS_S * D_STATE), F32),
                   jax.ShapeDtypeStruct((D_SSM, nb), F32),
                   jax.ShapeDtypeStruct((D_SSM, nb), F32)),
        grid=(1,),
        in_specs=[pl.BlockSpec((nb, CONV_DIM), lambda i: (0, COL_XBC // CONV_DIM)),
                  pl.BlockSpec((nb, CONV_DIM), lambda i: (0, 0)),
                  pl.BlockSpec((nb, CONV_DIM), lambda i: (0, 1)),
                  pl.BlockSpec((nb, CONV_DIM), lambda i: (0, 2)),
                  pl.BlockSpec((nb, LANE), lambda i: (0, COL_DT // LANE)),
                  pl.BlockSpec((SSM_CONV, CONV_DIM), const),
                  pl.BlockSpec((1, CONV_DIM), const),
                  pl.BlockSpec((1, LANE), const),
                  pl.BlockSpec((1, LANE), const),
                  pl.BlockSpec((LANE, D_SSM), const)],
        out_specs=(pl.BlockSpec((nb, D_SSM), const),
                   pl.BlockSpec((nb, 2 * N_GROUPS_S * D_STATE), const),
                   pl.BlockSpec((D_SSM, nb), const),
                   pl.BlockSpec((D_SSM, nb), const)),
        compiler_params=_cparams(("arbitrary",)),
        name="ssd_sample_prep",
    )(proj, conv_state, conv_state, conv_state, proj, lw['ssm_conv_w'], lw['ssm_conv_b'], lw['dt_bias'],
      lw['a_log'], lw['head_expand'])


SSD_STEP_BATCH = 8


def _ssd_sample_state_kernel(st_ref, xdt_t_ref, dec_t_ref, bc_ref, st_out, y_t_ref, *, bb):
    i0 = pl.program_id(0) * bb
    nb = xdt_t_ref.shape[1]
    lane = lax.broadcasted_iota(jnp.int32, (D_SSM, nb), 1)
    rows_per_group = D_SSM // N_GROUPS_S
    heads_per_group = N_HEADS_S // N_GROUPS_S

    @pl.when(i0 == 0)
    def _():
        y_t_ref[...] = jnp.zeros(y_t_ref.shape, F32)

    y_t = y_t_ref[...]
    for i in range(bb):
        b = i0 + i
        sel = lane == b
        xcol = jnp.sum(jnp.where(sel, xdt_t_ref[...], 0.0), axis=1, keepdims=True)
        dcol = jnp.sum(jnp.where(sel, dec_t_ref[...], 0.0), axis=1, keepdims=True)
        bc = bc_ref[pl.ds(b, 1), :]
        hs = st_ref[i].reshape(D_SSM, D_STATE)
        ycols = []
        for g in range(N_GROUPS_S):
            sl = slice(g * rows_per_group, (g + 1) * rows_per_group)
            brow = bc[:, g * D_STATE:(g + 1) * D_STATE]
            crow = bc[:, (N_GROUPS_S + g) * D_STATE:(N_GROUPS_S + g + 1) * D_STATE]
            new = hs[sl] * dcol[sl] + xcol[sl] * brow
            st_out[i, g * heads_per_group:(g + 1) * heads_per_group] = new.reshape(
                heads_per_group, SSM_HEAD_DIM, D_STATE)
            ycols.append(jnp.sum(new * crow, axis=1, keepdims=True))
        y_t = jnp.where(sel, jnp.concatenate(ycols, axis=0), y_t)
    y_t_ref[...] = y_t


def ssd_sample_state(li, state, xdt_t, dec_t, bc):
    nb = state.shape[1]
    bb = _tile(nb, SSD_STEP_BATCH)
    const = lambda b: (0, 0)
    st_spec = pl.BlockSpec((bb, N_HEADS_S, SSM_HEAD_DIM, D_STATE), lambda b: (b, 0, 0, 0))
    st_in_spec = pl.BlockSpec((None, bb, N_HEADS_S, SSM_HEAD_DIM, D_STATE), lambda b: (li, b, 0, 0, 0))
    return pl.pallas_call(
        functools.partial(_ssd_sample_state_kernel, bb=bb),
        out_shape=(jax.ShapeDtypeStruct(state.shape[1:], F32), jax.ShapeDtypeStruct((D_SSM, nb), F32)),
        grid=(nb // bb,),
        in_specs=[st_in_spec, pl.BlockSpec((D_SSM, nb), const), pl.BlockSpec((D_SSM, nb), const),
                  pl.BlockSpec((nb, 2 * N_GROUPS_S * D_STATE), const)],
        out_specs=(st_spec, pl.BlockSpec((D_SSM, nb), const)),
        compiler_params=_cparams(("arbitrary",)),
        name="ssd_sample_state",
    )(state, xdt_t, dec_t, bc)


def _sample_mix_kernel(y_t_ref, x_ref, z_ref, u_ref, v_ref, dsk_ref, gs_ref, gv_ref, wv_ref, bv_ref,
                       ssm_ref, cm_ref, vn_ref):
    yy = y_t_ref[...].T + x_ref[...] * dsk_ref[...]
    ssm_ref[...] = _rms(yy * _silu(z_ref[...]), gs_ref[...]).astype(BF16)
    vn = _rms(v_ref[...], gv_ref[...])
    vn_ref[...] = vn
    cm_ref[...] = (u_ref[...] * (vn * wv_ref[...] + bv_ref[...])).astype(BF16)


def sample_mix(y_t, x, proj, lw):
    nb = x.shape[0]
    const = lambda i: (0, 0)
    blk = lambda col: pl.BlockSpec((nb, D_SSM), lambda i, col=col: (0, col // D_SSM))
    vec = pl.BlockSpec((1, D_SSM), const)
    return pl.pallas_call(
        _sample_mix_kernel,
        out_shape=(jax.ShapeDtypeStruct((nb, D_SSM), BF16), jax.ShapeDtypeStruct((nb, D_CM), BF16),
                   jax.ShapeDtypeStruct((nb, D_CM), F32)),
        grid=(1,),
        in_specs=[pl.BlockSpec((D_SSM, nb), const), pl.BlockSpec((nb, D_SSM), const),
                  blk(COL_Z), blk(COL_U), blk(COL_V), vec, vec, vec, vec, vec],
        out_specs=(pl.BlockSpec((nb, D_SSM), const), pl.BlockSpec((nb, D_CM), const),
                   pl.BlockSpec((nb, D_CM), const)),
        compiler_params=_cparams(("arbitrary",)),
        name="sample_mix",
    )(y_t, x, proj, proj, proj, lw['d_skip'], lw['g_ssm_out'], lw['g_sgu'], lw['sgu_w0'], lw['sgu_b0'])


def _spatial_gate_kernel(u_ref, v_ref, g_ref, w_ref, bt_ref, o_ref, *, n_chunks):
    vn = _rms(v_ref[...], g_ref[...]).astype(BF16)
    ii = lax.broadcasted_iota(jnp.int32, (CHUNK, CHUNK), 0)
    jj = lax.broadcasted_iota(jnp.int32, (CHUNK, CHUNK), 1)
    bt = bt_ref[...]
    for g in range(N_CG):
        w = jnp.where(jj <= ii, w_ref[g], 0.0).astype(BF16)
        bcol = bt[:, g:g + 1]
        for c in range(n_chunks):
            rs = slice(c * CHUNK, (c + 1) * CHUNK)
            cs = slice(g * CG_DIM, (g + 1) * CG_DIM)
            mixed = _dot(w, vn[rs, cs]) + bcol
            o_ref[rs, cs] = (u_ref[rs, cs] * mixed).astype(BF16)


def spatial_gate(proj, lw, *, tm):
    r = proj.shape[0]
    const = lambda i: (0, 0)
    return pl.pallas_call(
        functools.partial(_spatial_gate_kernel, n_chunks=tm // CHUNK),
        out_shape=jax.ShapeDtypeStruct((r, D_CM), BF16),
        grid=(r // tm,),
        in_specs=[pl.BlockSpec((tm, D_CM), lambda i: (i, COL_U // D_CM)),
                  pl.BlockSpec((tm, D_CM), lambda i: (i, COL_V // D_CM)),
                  pl.BlockSpec((1, D_CM), const),
                  pl.BlockSpec((N_CG, CHUNK, CHUNK), lambda i: (0, 0, 0)),
                  pl.BlockSpec((CHUNK, N_CG), const)],
        out_specs=pl.BlockSpec((tm, D_CM), lambda i: (i, 0)),
        compiler_params=_cparams(("parallel",)),
        name="spatial_gate",
    )(proj, proj, lw['g_sgu'], lw['w_spatial'], lw['b_spatial_t'])


def _out_proj_kernel(x_ref, att_ref, ssm_ref, cm_ref, w_ref, o_ref):
    na = N_HEADS_A * V_DIM
    acc = _dot(att_ref[...], w_ref[0:na, :])
    acc = acc + _dot(ssm_ref[...], w_ref[na:na + D_SSM, :])
    acc = acc + _dot(cm_ref[...], w_ref[na + D_SSM:, :])
    o_ref[...] = x_ref[...] + acc


def out_proj(x, att, ssm, cm, w, *, tm, tn):
    r, d = x.shape
    w_mode = dict(pipeline_mode=pl.Buffered(1)) if tn == d else {}
    return pl.pallas_call(
        _out_proj_kernel,
        out_shape=jax.ShapeDtypeStruct((r, d), F32),
        grid=(r // tm, d // tn),
        in_specs=[pl.BlockSpec((tm, tn), lambda i, j: (i, j)),
                  pl.BlockSpec((tm, att.shape[1]), lambda i, j: (i, 0)),
                  pl.BlockSpec((tm, ssm.shape[1]), lambda i, j: (i, 0)),
                  pl.BlockSpec((tm, cm.shape[1]), lambda i, j: (i, 0)),
                  pl.BlockSpec((w.shape[0], tn), lambda i, j: (0, j), **w_mode)],
        out_specs=pl.BlockSpec((tm, tn), lambda i, j: (i, j)),
        compiler_params=_cparams(("parallel", "parallel")),
        name="out_proj",
    )(x, att, ssm, cm, w)


FFN_HALO = 16


def _ffn_prompt_kernel(x_ref, xp_ref, g_ref, wg_ref, wu_ref, cw_ref, cb_ref, wd_ref, o_ref, h_ref, gbuf, acc_ref,
                       *, tm, seq):
    i = pl.program_id(0)
    j = pl.program_id(1)

    @pl.when(j == 0)
    def _():
        keep = ((i * tm) % seq != 0).astype(F32)
        h_ref[0:FFN_HALO, :] = (_rms(xp_ref[...], g_ref[...]) * keep).astype(BF16)
        h_ref[FFN_HALO:, :] = _rms(x_ref[...], g_ref[...]).astype(BF16)
        acc_ref[...] = jnp.zeros(acc_ref.shape, F32)

    gbuf[...] = _dot(h_ref[...], wg_ref[...])
    up = _dot(h_ref[FFN_HALO:, :], wu_ref[...])
    cw = cw_ref[...]
    y = cb_ref[...]
    for t in range(FFN_CONV):
        off = FFN_HALO - (FFN_CONV - 1) + t
        y = y + cw[t:t + 1, :] * gbuf[off:off + tm, :]
    acc_ref[...] += _dot((_silu(y) * up).astype(BF16), wd_ref[...])

    @pl.when(j == pl.num_programs(1) - 1)
    def _():
        o_ref[...] = x_ref[...] + acc_ref[...]


def ffn_prompt(x, seq, lw, *, tm, tf):
    r, d = x.shape
    const = lambda i, j: (0, 0)
    halo_blocks = tm // FFN_HALO
    return pl.pallas_call(
        functools.partial(_ffn_prompt_kernel, tm=tm, seq=seq),
        out_shape=jax.ShapeDtypeStruct((r, d), F32),
        grid=(r // tm, D_FF // tf),
        in_specs=[pl.BlockSpec((tm, d), lambda i, j: (i, 0)),
                  pl.BlockSpec((FFN_HALO, d), lambda i, j: (jnp.maximum(i * halo_blocks - 1, 0), 0)),
                  pl.BlockSpec((1, d), const),
                  pl.BlockSpec((d, tf), lambda i, j: (0, j)),
                  pl.BlockSpec((d, tf), lambda i, j: (0, D_FF // tf + j)),
                  pl.BlockSpec((FFN_CONV, tf), lambda i, j: (0, j)),
                  pl.BlockSpec((1, tf), lambda i, j: (0, j)),
                  pl.BlockSpec((tf, d), lambda i, j: (j, 0))],
        out_specs=pl.BlockSpec((tm, d), lambda i, j: (i, 0)),
        scratch_shapes=[pltpu.VMEM((tm + FFN_HALO, d), BF16), pltpu.VMEM((tm + FFN_HALO, tf), F32),
                        pltpu.VMEM((tm, d), F32)],
        compiler_params=_cparams(("parallel", "arbitrary")),
        name="ffn_prompt",
    )(x, x, lw['g_ffn'], lw['w_gu'], lw['w_gu'], lw['ffn_conv_w'], lw['ffn_conv_b'], lw['w_down'])


def _ffn_sample_kernel(x_ref, p0_ref, p1_ref, g_ref, wg_ref, wu_ref, cw_ref, cb_ref, wd_ref, o_ref, gate_ref,
                       h_ref, acc_ref):
    j = pl.program_id(0)

    @pl.when(j == 0)
    def _():
        h_ref[...] = _rms(x_ref[...], g_ref[...]).astype(BF16)
        acc_ref[...] = jnp.zeros(acc_ref.shape, F32)

    gate = _dot(h_ref[...], wg_ref[...])
    up = _dot(h_ref[...], wu_ref[...])
    gate_ref[...] = gate
    cw = cw_ref[...]
    y = cb_ref[...] + cw[0:1] * p0_ref[...] + cw[1:2] * p1_ref[...] + cw[2:3] * gate
    acc_ref[...] += _dot((_silu(y) * up).astype(BF16), wd_ref[...])

    @pl.when(j == pl.num_programs(0) - 1)
    def _():
        o_ref[...] = x_ref[...] + acc_ref[...]


def ffn_sample(x, prev, lw, *, tf):
    nb, d = x.shape
    nf = D_FF // tf
    return pl.pallas_call(
        _ffn_sample_kernel,
        out_shape=(jax.ShapeDtypeStruct((nb, d), F32), jax.ShapeDtypeStruct((nb, D_FF), F32)),
        grid=(nf,),
        in_specs=[pl.BlockSpec((nb, d), lambda j: (0, 0)),
                  pl.BlockSpec((nb, tf), lambda j: (0, j)),
                  pl.BlockSpec((nb, tf), lambda j: (0, nf + j)),
                  pl.BlockSpec((1, d), lambda j: (0, 0)),
                  pl.BlockSpec((d, tf), lambda j: (0, j)),
                  pl.BlockSpec((d, tf), lambda j: (0, nf + j)),
                  pl.BlockSpec((FFN_CONV, tf), lambda j: (0, j)),
                  pl.BlockSpec((1, tf), lambda j: (0, j)),
                  pl.BlockSpec((tf, d), lambda j: (j, 0))],
        out_specs=(pl.BlockSpec((nb, d), lambda j: (0, 0)), pl.BlockSpec((nb, tf), lambda j: (0, j))),
        scratch_shapes=[pltpu.VMEM((nb, d), BF16), pltpu.VMEM((nb, d), F32)],
        compiler_params=_cparams(("arbitrary",)),
        name="ffn_sample",
    )(x, prev, prev, lw['g_ffn'], lw['w_gu'], lw['w_gu'], lw['ffn_conv_w'], lw['ffn_conv_b'], lw['w_down'])


def _rope_tables(pos):
    inv = ROPE_THETA ** (-jnp.arange(0, ROPE_DIM, 2, dtype=F32) / ROPE_DIM)
    ang = pos.astype(F32)[:, None] * inv[None, :]
    cos, sin = jnp.cos(ang), jnp.sin(ang)
    zero = jnp.zeros_like(cos)
    return (jnp.concatenate([cos, cos, zero, zero], axis=1),
            jnp.concatenate([-sin, zero, zero, zero], axis=1),
            jnp.concatenate([zero, sin, zero, zero], axis=1))


def _pack_layer(li, p):
    d = D_MODEL
    w_in = p['w_in'][li]
    offs = np.cumsum([0, Q_LORA, KV_LORA, ROPE_DIM, D_SSM, CONV_DIM, N_HEADS_S, D_CM, D_CM])
    seg = lambda k: w_in[:, int(offs[k]):int(offs[k + 1])]
    zpad = lambda n: jnp.zeros((d, n), w_in.dtype)
    w_in_p = jnp.concatenate([seg(0), seg(3), seg(6), seg(7), seg(4), seg(1),
                              seg(2), zpad(LANE - ROPE_DIM), seg(5), zpad(LANE - N_HEADS_S)], axis=1).astype(BF16)
    h = N_HEADS_A
    w_uq = jnp.pad(p['w_uq'][li], ((0, 0), (0, 0), (0, QK_PAD - QK_DIM))).reshape(Q_LORA, h * QK_PAD).astype(BF16)
    w_uk = p['w_uk'][li].reshape(KV_LORA, h * NOPE_DIM).astype(BF16)
    w_uv = p['w_uv'][li].reshape(KV_LORA, h * V_DIM).astype(BF16)
    pad_g = lambda g: jnp.pad(g, (0, QK_PAD - QK_DIM))[None, :]
    row = lambda v: v[None, :]
    lane_pad = lambda v: jnp.pad(v, (0, LANE - v.shape[0]))[None, :]
    head_expand = np.zeros((LANE, D_SSM), np.float32)
    head_expand[:N_HEADS_S] = np.kron(np.eye(N_HEADS_S, dtype=np.float32), np.ones((1, SSM_HEAD_DIM), np.float32))
    head_expand = jnp.asarray(head_expand)
    w_gu = p['w_gate_up'][li]
    return {
        'g_mix': row(p['g_mix'][li]), 'w_in': w_in_p,
        'g_q_a': row(p['g_q_a'][li]), 'g_kv_a': row(p['g_kv_a'][li]),
        'w_uq': w_uq, 'w_uk': w_uk, 'w_uk_t': w_uk.T, 'w_uv': w_uv,
        'gq': pad_g(p['g_qk_q'][li]) * (QK_DIM ** -0.5),
        'gk': pad_g(p['g_qk_k'][li]),
        'ssm_conv_w': p['ssm_conv_w'][li], 'ssm_conv_b': row(p['ssm_conv_b'][li]),
        'dt_bias': lane_pad(p['ssm_dt_bias'][li]), 'a_log': lane_pad(p['ssm_a_log'][li]),
        'd_skip': row(jnp.repeat(p['ssm_d'][li], SSM_HEAD_DIM)),
        'g_ssm_out': row(p['g_ssm_out'][li]), 'head_expand': head_expand,
        'g_sgu': row(p['g_sgu'][li]), 'w_spatial': p['w_spatial'][li], 'b_spatial_t': p['b_spatial'][li].T,
        'sgu_w0': row(jnp.repeat(p['w_spatial'][li][:, 0, 0], CG_DIM)),
        'sgu_b0': row(jnp.repeat(p['b_spatial'][li][:, 0], CG_DIM)),
        'w_out': p['w_out'][li].astype(BF16),
        'g_ffn': row(p['g_ffn'][li]),
        'w_gu': w_gu.astype(BF16),
        'ffn_conv_w': p['ffn_conv_w'][li], 'ffn_conv_b': row(p['ffn_conv_b'][li]),
        'w_down': p['w_down'][li].astype(BF16),
    }


def _tile(n, pref):
    t = min(n, pref)
    assert n % t == 0, (n, pref)
    return t


def kernel(x_prompt, x_sample, cache_latent, cache_krope, state_ssm, state_conv, state_ffn_conv, page_table, g_mix, w_in, g_q_a, w_uq, g_kv_a, w_uk, w_uv, g_qk_q, g_qk_k, ssm_conv_w, ssm_conv_b, ssm_dt_bias, ssm_a_log, ssm_d, g_ssm_out, g_sgu, w_spatial, b_spatial, w_out, g_ffn, w_gate_up, ffn_conv_w, ffn_conv_b, w_down):
    params = dict(g_mix=g_mix, w_in=w_in, g_q_a=g_q_a, w_uq=w_uq, g_kv_a=g_kv_a, w_uk=w_uk, w_uv=w_uv,
                  g_qk_q=g_qk_q, g_qk_k=g_qk_k, ssm_conv_w=ssm_conv_w, ssm_conv_b=ssm_conv_b,
                  ssm_dt_bias=ssm_dt_bias, ssm_a_log=ssm_a_log, ssm_d=ssm_d, g_ssm_out=g_ssm_out, g_sgu=g_sgu,
                  w_spatial=w_spatial, b_spatial=b_spatial, w_out=w_out, g_ffn=g_ffn, w_gate_up=w_gate_up,
                  ffn_conv_w=ffn_conv_w, ffn_conv_b=ffn_conv_b, w_down=w_down)
    depth = w_in.shape[0]
    bp, seq, d = x_prompt.shape
    bd, dec_seq, _ = x_sample.shape
    assert dec_seq == 1 and d == D_MODEL and cache_latent.shape[2] == PAGE
    n_pages = page_table.shape[1]
    past_len = n_pages * PAGE
    rp = bp * seq

    tabs_p = _rope_tables(jnp.arange(seq, dtype=jnp.int32))
    tabs_s = _rope_tables(jnp.full((bd,), past_len, jnp.int32))

    tm_p = _tile(seq, 512)
    tq = _tile(seq, 1024)
    tk = tq
    pp = next(c for c in (64, 32, 16, 8, 4, 2) if n_pages % c == 0)
    cache_krope_t = jnp.swapaxes(cache_krope, 2, 3)
    tf = 512

    yp = x_prompt.reshape(rp, d)
    ys = x_sample.reshape(bd, d)
    outs = [[] for _ in range(11)]
    for li in range(depth):
        lw = _pack_layer(li, params)

        proj = norm_matmul(yp, lw['g_mix'], lw['w_in'], tm=tm_p, tn=IN_PAD)
        q, k, v, lat, kro = mla_prep(proj, bp, seq, lw, lw['gq'] * LOG2E, tabs_p, tm=_tile(seq, 256), q_dtype=BF16)
        att = flash_attention(q, k, v, tq=tq, tk=tk)
        ssm, ssm_state = ssd_prompt(proj, bp, seq, lw)
        cm = spatial_gate(proj, lw, tm=tm_p)
        y_mid = out_proj(yp, att.reshape(rp, -1), ssm, cm, lw['w_out'], tm=tm_p, tn=d)
        yp = ffn_prompt(y_mid, seq, lw, tm=tm_p, tf=tf)
        tail = y_mid.reshape(bp, seq, d)[:, seq - (FFN_CONV - 1):, :].reshape(bp * (FFN_CONV - 1), d)
        tail = jnp.pad(tail, ((0, 8 - tail.shape[0]), (0, 0)))
        gate_tail = norm_matmul(tail, lw['g_ffn'], lw['w_gu'], tm=8, tn=tf, n=D_FF)[:bp * (FFN_CONV - 1)]
        proj3 = proj.reshape(bp, seq, IN_PAD)
        outs[0].append(lat)
        outs[1].append(kro)
        outs[2].append(ssm_state)
        outs[3].append(proj3[:, seq - (SSM_CONV - 1):, COL_XBC:COL_XBC + CONV_DIM])
        outs[4].append(gate_tail.reshape(bp, FFN_CONV - 1, D_FF))

        proj_s = norm_matmul(ys, lw['g_mix'], lw['w_in'], tm=bd, tn=IN_PAD)
        q_s, k_s, _, lat_s, kro_s = mla_prep(proj_s, 1, bd, lw, lw['gq'], tabs_s, tm=bd, q_dtype=F32)
        qabs, qgr = dec_q(q_s, lw['gk'], lw['w_uk'])
        m, l, acc = dec_paged(li, page_table, cache_latent, cache_krope_t, lw['w_uk_t'],
                              qabs.transpose(1, 0, 2), qgr.transpose(1, 0, 2), pp=pp)
        att_s = dec_finish(m.transpose(1, 0, 2), l.transpose(1, 0, 2), acc.transpose(1, 0, 2), q_s, k_s,
                           lat_s[0], lw['w_uv'])
        conv_prev = state_conv[li]
        x_s, bc_s, xdt_t, dec_t = ssd_sample_prep(proj_s, conv_prev.reshape(bd, -1), lw)
        ssm_state_s, y_t = ssd_sample_state(li, state_ssm, xdt_t, dec_t, bc_s)
        ssm_s, cm_s, vn_s = sample_mix(y_t, x_s, proj_s, lw)
        ys_mid = out_proj(ys, att_s, ssm_s, cm_s, lw['w_out'], tm=bd, tn=d)
        ffn_prev = state_ffn_conv[li]
        ys, gate_s = ffn_sample(ys_mid, ffn_prev.reshape(bd, -1), lw, tf=tf)
        outs[5].append(lat_s.reshape(bd, 1, KV_LORA))
        outs[6].append(kro_s.reshape(bd, 1, ROPE_DIM))
        outs[7].append(ssm_state_s)
        outs[8].append(jnp.concatenate([conv_prev[:, 1:], proj_s[:, None, COL_XBC:COL_XBC + CONV_DIM]], axis=1))
        outs[9].append(jnp.concatenate([ffn_prev[:, 1:], gate_s[:, None, :]], axis=1))
        outs[10].append(vn_s.reshape(bd, 1, D_CM))

    stacked = [jnp.stack(o) for o in outs]
    return (yp.reshape(bp, seq, d), ys.reshape(bd, 1, d), *stacked)
```

```python
import functools

import numpy as np
import jax
import jax.numpy as jnp
from jax import lax
from jax.experimental import pallas as pl
from jax.experimental.pallas import tpu as pltpu

F32 = jnp.float32
BF16 = jnp.bfloat16
EPS = 1e-6
HIGHEST = lax.Precision.HIGHEST

D_MODEL = 2048
N_HEADS_A = 8
NOPE_DIM = 128
ROPE_DIM = 64
QK_DIM = NOPE_DIM + ROPE_DIM
QK_PAD = 256
V_DIM = 128
V_PAD = 256
Q_LORA = 512
KV_LORA = 256
ROPE_THETA = 10000.0
D_SSM = 512
SSM_HEAD_DIM = 64
N_HEADS_S = 8
N_GROUPS_S = 2
D_STATE = 128
SSM_CONV = 4
SSD_CHUNK = 128
CONV_DIM = 1024
D_CM = 512
CHUNK = 128
N_CG = 4
CG_DIM = 128
D_FF = 5632
FFN_CONV = 3
PAGE = 128

COL_CQ, COL_Z, COL_U, COL_V, COL_XBC, COL_CKV, COL_KR, COL_DT = 0, 512, 1024, 1536, 2048, 3072, 3328, 3456
IN_PAD = 3584

LANE = 128
VMEM_LIMIT = 56 * 1024 * 1024

NT_DIMS = (((1,), (1,)), ((), ()))


def _cparams(sem):
    return pltpu.CompilerParams(dimension_semantics=sem, vmem_limit_bytes=VMEM_LIMIT)


def _rms(x, g):
    return x * lax.rsqrt(jnp.mean(x * x, axis=-1, keepdims=True) + EPS) * g


def _silu(x):
    return x * (1.0 / (1.0 + jnp.exp(-x)))


def _softplus(x):
    return jnp.maximum(x, 0.0) + jnp.log(1.0 + jnp.exp(-jnp.abs(x)))


def _dot(a, b):
    return jnp.dot(a, b, preferred_element_type=F32)


def _dot_nt(a, b):
    return lax.dot_general(a, b, NT_DIMS, preferred_element_type=F32)


def _norm_matmul_kernel(x_ref, g_ref, w_ref, o_ref, h_ref):
    @pl.when(pl.program_id(1) == 0)
    def _():
        h_ref[...] = _rms(x_ref[...], g_ref[...]).astype(BF16)

    o_ref[...] = _dot(h_ref[...], w_ref[...])


def norm_matmul(x, g, w, li, *, tm, tn, n=None):
    r, k = x.shape
    n = w.shape[2] if n is None else n
    w_mode = dict(pipeline_mode=pl.Buffered(1)) if tn == n else {}
    return pl.pallas_call(
        _norm_matmul_kernel,
        out_shape=jax.ShapeDtypeStruct((r, n), F32),
        grid=(r // tm, n // tn),
        in_specs=[pl.BlockSpec((tm, k), lambda i, j: (i, 0)),
                  pl.BlockSpec((1, k), lambda i, j: (0, 0)),
                  pl.BlockSpec((None, k, tn), lambda i, j: (li, 0, j), **w_mode)],
        out_specs=pl.BlockSpec((tm, tn), lambda i, j: (i, j)),
        scratch_shapes=[pltpu.VMEM((tm, k), BF16)],
        compiler_params=_cparams(("parallel", "arbitrary")),
        name="norm_matmul",
    )(x, g, w)


def _mla_prep_kernel(cq_ref, ckv_ref, kr_ref, gqa_ref, gkva_ref, wuq_ref, wuk_ref, wuv_ref, gq_ref, gk_ref,
                     cos_ref, s1_ref, s2_ref, q_ref, k_ref, v_ref, lat_ref, kro_ref):
    cos = cos_ref[...]
    s1 = s1_ref[...]
    s2 = s2_ref[...]

    def rope(b):
        return b * cos + pltpu.roll(b, 96, 1) * s1 + pltpu.roll(b, 32, 1) * s2

    def head_norm(a, b, g):
        ss = jnp.sum(a * a + b * b, axis=-1, keepdims=True)
        inv = lax.rsqrt(ss * (1.0 / QK_DIM) + EPS)
        return a * inv * g[:, :NOPE_DIM], b * inv * g[:, NOPE_DIM:]

    cqn = _rms(cq_ref[...], gqa_ref[...]).astype(BF16)
    qf = _dot(cqn, wuq_ref[...])
    gq = gq_ref[...]
    for h in range(N_HEADS_A):
        a = qf[:, h * QK_PAD:h * QK_PAD + NOPE_DIM]
        b = rope(qf[:, h * QK_PAD + NOPE_DIM:(h + 1) * QK_PAD])
        a, b = head_norm(a, b, gq)
        q_ref[0, h, :, 0:NOPE_DIM] = a.astype(q_ref.dtype)
        q_ref[0, h, :, NOPE_DIM:QK_PAD] = b.astype(q_ref.dtype)

    ckvn = _rms(ckv_ref[...], gkva_ref[...])
    lat_ref[0] = ckvn
    cb = ckvn.astype(BF16)
    kn = _dot(cb, wuk_ref[...])
    vf = _dot(cb, wuv_ref[...])
    krr = rope(kr_ref[...])
    kro_ref[0] = krr[:, :ROPE_DIM]
    gk = gk_ref[...]
    lane = lax.broadcasted_iota(jnp.int32, (vf.shape[0], V_PAD - V_DIM), 1)
    ones_col = jnp.where(lane == 0, 1.0, 0.0).astype(BF16)
    for h in range(N_HEADS_A):
        a, b = head_norm(kn[:, h * NOPE_DIM:(h + 1) * NOPE_DIM], krr, gk)
        k_ref[0, h, :, 0:NOPE_DIM] = a.astype(BF16)
        k_ref[0, h, :, NOPE_DIM:QK_PAD] = b.astype(BF16)
        v_ref[0, h, :, 0:V_DIM] = vf[:, h * V_DIM:(h + 1) * V_DIM].astype(BF16)
        v_ref[0, h, :, V_DIM:V_PAD] = ones_col


def mla_prep(proj, nb, seq, lw, gq, tabs, *, tm, q_dtype):
    nt = seq // tm
    row = lambda b, t: (b * nt + t, 0)
    const = lambda b, t: (0, 0)
    tab = lambda b, t: (t, 0)
    h = N_HEADS_A
    return pl.pallas_call(
        _mla_prep_kernel,
        out_shape=(jax.ShapeDtypeStruct((nb, h, seq, QK_PAD), q_dtype),
                   jax.ShapeDtypeStruct((nb, h, seq, QK_PAD), BF16),
                   jax.ShapeDtypeStruct((nb, h, seq, V_PAD), BF16),
                   jax.ShapeDtypeStruct((nb, seq, KV_LORA), F32),
                   jax.ShapeDtypeStruct((nb, seq, ROPE_DIM), F32)),
        grid=(nb, nt),
        in_specs=[pl.BlockSpec((tm, Q_LORA), lambda b, t: (b * nt + t, COL_CQ // Q_LORA)),
                  pl.BlockSpec((tm, KV_LORA), lambda b, t: (b * nt + t, COL_CKV // KV_LORA)),
                  pl.BlockSpec((tm, LANE), lambda b, t: (b * nt + t, COL_KR // LANE)),
                  pl.BlockSpec((1, Q_LORA), const),
                  pl.BlockSpec((1, KV_LORA), const),
                  pl.BlockSpec((Q_LORA, h * QK_PAD), const),
                  pl.BlockSpec((KV_LORA, h * NOPE_DIM), const),
                  pl.BlockSpec((KV_LORA, h * V_DIM), const),
                  pl.BlockSpec((1, QK_PAD), const),
                  pl.BlockSpec((1, QK_PAD), const),
                  pl.BlockSpec((tm, LANE), tab),
                  pl.BlockSpec((tm, LANE), tab),
                  pl.BlockSpec((tm, LANE), tab)],
        out_specs=(pl.BlockSpec((1, h, tm, QK_PAD), lambda b, t: (b, 0, t, 0)),
                   pl.BlockSpec((1, h, tm, QK_PAD), lambda b, t: (b, 0, t, 0)),
                   pl.BlockSpec((1, h, tm, V_PAD), lambda b, t: (b, 0, t, 0)),
                   pl.BlockSpec((1, tm, KV_LORA), lambda b, t: (b, t, 0)),
                   pl.BlockSpec((1, tm, ROPE_DIM), lambda b, t: (b, t, 0))),
        compiler_params=_cparams(("parallel", "parallel")),
        name="mla_prep",
    )(proj, proj, proj, lw['g_q_a'], lw['g_kv_a'], lw['w_uq'], lw['w_uk'], lw['w_uv'], gq, lw['gk'],
      tabs[0], tabs[1], tabs[2])


FLASH_ROWS = 256
LOG2E = 1.4426950408889634


def _flash_kernel(qi_ref, ki_ref, q_ref, k_ref, v_ref, o_ref, m_ref, acc_ref, *, tq, tk):
    p = pl.program_id(2)
    qi = qi_ref[p]
    ki = ki_ref[p]

    @pl.when(ki == 0)
    def _():
        m_ref[...] = jnp.full(m_ref.shape, -jnp.inf, F32)
        acc_ref[...] = jnp.zeros(acc_ref.shape, F32)

    n_sub = tq // FLASH_ROWS

    def scores(r, n_cols):
        rows = slice(r * FLASH_ROWS, (r + 1) * FLASH_ROWS)
        return _dot_nt(q_ref[0, 0, rows, :], k_ref[0, 0, 0:n_cols, :])

    def update(r, s, masked):
        rows = slice(r * FLASH_ROWS, (r + 1) * FLASH_ROWS)
        n_cols = s.shape[1]
        if masked:
            row = r * FLASH_ROWS + lax.broadcasted_iota(jnp.int32, s.shape, 0)
            col = lax.broadcasted_iota(jnp.int32, s.shape, 1)
            s = jnp.where(col <= row, s, -jnp.inf)
        m_prev = m_ref[rows, :]
        m_new = jnp.maximum(m_prev, jnp.max(s, axis=-1, keepdims=True))
        alpha = jnp.exp2(m_prev - m_new)
        pr = jnp.exp2(s - m_new)
        acc_ref[rows, :] = alpha * acc_ref[rows, :] + _dot(pr.astype(BF16), v_ref[0, 0, 0:n_cols, :])
        m_ref[rows, :] = m_new

    def sweep(n_cols_of, masked):
        s_next = scores(0, n_cols_of(0))
        for r in range(n_sub):
            s = s_next
            if r + 1 < n_sub:
                s_next = scores(r + 1, n_cols_of(r + 1))
            update(r, s, masked)

    @pl.when(ki < qi)
    def _():
        sweep(lambda r: tk, False)

    @pl.when(ki == qi)
    def _():
        sweep(lambda r: (r + 1) * FLASH_ROWS, True)
        o_ref[0] = (acc_ref[:, 0:V_DIM] / acc_ref[:, V_DIM:V_DIM + 1]).astype(BF16)


def flash_attention(q, k, v, *, tq, tk):
    b, h, s, _ = q.shape
    assert tq == tk and tq % FLASH_ROWS == 0
    pairs = [(i, j) for i in range(s // tq) for j in range(i + 1)]
    qi_tab = jnp.asarray(np.array([p[0] for p in pairs], np.int32))
    ki_tab = jnp.asarray(np.array([p[1] for p in pairs], np.int32))
    return pl.pallas_call(
        functools.partial(_flash_kernel, tq=tq, tk=tk),
        out_shape=jax.ShapeDtypeStruct((b, s, h * V_DIM), BF16),
        grid_spec=pltpu.PrefetchScalarGridSpec(
            num_scalar_prefetch=2,
            grid=(b, h, len(pairs)),
            in_specs=[pl.BlockSpec((1, 1, tq, QK_PAD), lambda bi, hi, p, qt, kt: (bi, hi, qt[p], 0)),
                      pl.BlockSpec((1, 1, tk, QK_PAD), lambda bi, hi, p, qt, kt: (bi, hi, kt[p], 0)),
                      pl.BlockSpec((1, 1, tk, V_PAD), lambda bi, hi, p, qt, kt: (bi, hi, kt[p], 0))],
            out_specs=pl.BlockSpec((1, tq, V_DIM), lambda bi, hi, p, qt, kt: (bi, qt[p], hi)),
            scratch_shapes=[pltpu.VMEM((tq, 1), F32), pltpu.VMEM((tq, V_PAD), F32)]),
        compiler_params=_cparams(("parallel", "parallel", "arbitrary")),
        name="flash_attention",
    )(qi_tab, ki_tab, q, k, v)


def _dec_q_kernel(q_ref, gk_ref, wuk_ref, qabs_ref, qgr_ref):
    gk = gk_ref[...]
    nb = q_ref.shape[2]
    for h in range(N_HEADS_A):
        qg = q_ref[0, h] * gk
        qn = qg[:, :NOPE_DIM].astype(BF16)
        qabs_ref[h] = _dot_nt(qn, wuk_ref[:, h * NOPE_DIM:(h + 1) * NOPE_DIM]).astype(BF16)
        qgr_ref[h] = qg[:, NOPE_DIM:QK_DIM].astype(BF16)
    for h in range(N_HEADS_A, 2 * N_HEADS_A):
        qabs_ref[h] = jnp.zeros((nb, KV_LORA), BF16)
        qgr_ref[h] = jnp.zeros((nb, ROPE_DIM), BF16)


def dec_q(q, gk, wuk):
    nb = q.shape[2]
    return pl.pallas_call(
        _dec_q_kernel,
        out_shape=(jax.ShapeDtypeStruct((2 * N_HEADS_A, nb, KV_LORA), BF16),
                   jax.ShapeDtypeStruct((2 * N_HEADS_A, nb, ROPE_DIM), BF16)),
        compiler_params=pltpu.CompilerParams(vmem_limit_bytes=VMEM_LIMIT),
        name="dec_q",
    )(q, gk, wuk)


DEC_SLOTS = 3


def _dec_paged_kernel(pt_ref, lat_hbm, krt_hbm, wukt_ref, qabs_ref, qgr_ref, m_out, l_out, acc_out,
                      latbuf, krbuf, sem, lhs_ref, cbuf, m_s, l_s, acc_s, *, li, pp):
    j = pl.program_id(1)
    nj = pl.num_programs(1)
    n_steps = pl.num_programs(0) * nj
    step = pl.program_id(0) * nj + j
    slot = step % DEC_SLOTS
    ahead = [(lax.rem(step + a, n_steps), lax.rem(step + a, DEC_SLOTS)) for a in (1, 2)]
    nh = N_HEADS_A
    n_k = nh * NOPE_DIM

    def page_copies(st, sl):
        row = st // nj
        col = (st % nj) * pp
        copies = []
        for i in range(pp):
            pid = pt_ref[row, col + i]
            copies.append(pltpu.make_async_copy(lat_hbm.at[li, pid], latbuf.at[sl, i], sem.at[0, sl]))
            copies.append(pltpu.make_async_copy(krt_hbm.at[li, pid], krbuf.at[sl, i], sem.at[1, sl]))
        return copies

    @pl.when(step == 0)
    def _():
        for c in page_copies(step, slot) + page_copies(*ahead[0]):
            c.start()

    @pl.when(j == 0)
    def _():
        lhs_ref[0:n_k, :] = wukt_ref[...]
        lhs_ref[n_k:n_k + 2 * nh, :] = qabs_ref[0]
        m_s[...] = jnp.full(m_s.shape, -jnp.inf, F32)
        l_s[...] = jnp.zeros(l_s.shape, F32)
        acc_s[...] = jnp.zeros(acc_s.shape, F32)

    for c in page_copies(step, slot):
        c.wait()

    qgr = qgr_ref[0]
    parts = []
    for t in range(pp // 2):
        cb = jnp.concatenate([latbuf[slot, 2 * t], latbuf[slot, 2 * t + 1]], axis=0).astype(BF16)
        cbuf[t * 2 * PAGE:(t + 1) * 2 * PAGE, :] = cb
        kt = _dot_nt(lhs_ref[...], cb)
        k3 = kt[:n_k].reshape(nh, NOPE_DIM, 2 * PAGE)
        ssq = jnp.sum(k3 * k3, axis=1)
        krt = jnp.concatenate([krbuf[slot, 2 * t], krbuf[slot, 2 * t + 1]], axis=1)
        ssq = ssq + jnp.sum(krt * krt, axis=0, keepdims=True)
        num = kt[n_k:n_k + nh] + _dot(qgr, krt.astype(BF16))[:nh]
        parts.append(num * lax.rsqrt(ssq * (1.0 / QK_DIM) + EPS))
    s = jnp.concatenate(parts, axis=1)
    m_prev = m_s[...]
    m_new = jnp.maximum(m_prev, jnp.max(s, axis=-1, keepdims=True))
    alpha = jnp.exp(m_prev - m_new)
    p = jnp.exp(s - m_new)
    l_s[...] = alpha * l_s[...] + jnp.sum(p, axis=-1, keepdims=True)
    pb = jnp.concatenate([p, jnp.zeros_like(p)], axis=0).astype(BF16)
    for c in page_copies(*ahead[1]):
        c.start()
    acc_s[...] = alpha * acc_s[...] + _dot(pb, cbuf[...])[:nh]
    m_s[...] = m_new

    @pl.when(j == pl.num_programs(1) - 1)
    def _():
        m_out[0] = jnp.broadcast_to(m_s[...], (nh, LANE))
        l_out[0] = jnp.broadcast_to(l_s[...], (nh, LANE))
        acc_out[0] = acc_s[...]

    @pl.when(step == n_steps - 1)
    def _():
        for c in page_copies(*ahead[0]) + page_copies(*ahead[1]):
            c.wait()


def dec_paged(li, page_table, cache_lat, cache_kr_t, wukt, qabs, qgr, *, pp):
    nb, n_pages = page_table.shape
    nh = N_HEADS_A
    return pl.pallas_call(
        functools.partial(_dec_paged_kernel, li=li, pp=pp),
        out_shape=(jax.ShapeDtypeStruct((nb, nh, LANE), F32),
                   jax.ShapeDtypeStruct((nb, nh, LANE), F32),
                   jax.ShapeDtypeStruct((nb, nh, KV_LORA), F32)),
        grid_spec=pltpu.PrefetchScalarGridSpec(
            num_scalar_prefetch=1,
            grid=(nb, n_pages // pp),
            in_specs=[
                pl.BlockSpec(memory_space=pl.ANY),
                pl.BlockSpec(memory_space=pl.ANY),
                pl.BlockSpec((nh * NOPE_DIM, KV_LORA), lambda b, j, pt: (0, 0)),
                pl.BlockSpec((1, 2 * nh, KV_LORA), lambda b, j, pt: (b, 0, 0)),
                pl.BlockSpec((1, 2 * nh, ROPE_DIM), lambda b, j, pt: (b, 0, 0))],
            out_specs=(pl.BlockSpec((1, nh, LANE), lambda b, j, pt: (b, 0, 0)),
                       pl.BlockSpec((1, nh, LANE), lambda b, j, pt: (b, 0, 0)),
                       pl.BlockSpec((1, nh, KV_LORA), lambda b, j, pt: (b, 0, 0))),
            scratch_shapes=[pltpu.VMEM((DEC_SLOTS, pp, PAGE, KV_LORA), F32),
                            pltpu.VMEM((DEC_SLOTS, pp, ROPE_DIM, PAGE), F32),
                            pltpu.SemaphoreType.DMA((2, DEC_SLOTS)),
                            pltpu.VMEM((nh * NOPE_DIM + 2 * nh, KV_LORA), BF16),
                            pltpu.VMEM((pp * PAGE, KV_LORA), BF16),
                            pltpu.VMEM((nh, 1), F32), pltpu.VMEM((nh, 1), F32), pltpu.VMEM((nh, KV_LORA), F32)]),
        compiler_params=_cparams(("arbitrary", "arbitrary")),
        name="dec_paged",
    )(page_table, cache_lat, cache_kr_t, wukt, qabs, qgr)


def _dec_finish_kernel(m_ref, l_ref, acc_ref, q_ref, k_ref, lat_ref, wuv_ref, o_ref):
    cn = lat_ref[...].astype(BF16).astype(F32)
    for h in range(N_HEADS_A):
        qh = q_ref[0, h].astype(BF16).astype(F32)
        s_new = jnp.sum(qh * k_ref[0, h].astype(F32), axis=-1, keepdims=True)
        m_h = m_ref[h][:, 0:1]
        l_h = l_ref[h][:, 0:1]
        m_tot = jnp.maximum(m_h, s_new)
        a = jnp.exp(m_h - m_tot)
        pn = jnp.exp(s_new - m_tot)
        lat = (acc_ref[h] * a + pn * cn) / (l_h * a + pn)
        o_ref[:, h * V_DIM:(h + 1) * V_DIM] = _dot(lat.astype(BF16), wuv_ref[:, h * V_DIM:(h + 1) * V_DIM]).astype(BF16)


def dec_finish(m, l, acc, q, k, lat, wuv):
    nb = lat.shape[0]
    return pl.pallas_call(
        _dec_finish_kernel,
        out_shape=jax.ShapeDtypeStruct((nb, N_HEADS_A * V_DIM), BF16),
        compiler_params=pltpu.CompilerParams(vmem_limit_bytes=VMEM_LIMIT),
        name="dec_finish",
    )(m, l, acc, q, k, lat, wuv)


def _ssd_prompt_kernel(xbc_ref, dt_ref, z_ref, cw_ref, cb_ref, dtb_ref, alog_ref, dsk_ref, g_ref, exp_ref,
                       y_ref, st_ref, xbuf, state):
    c = pl.program_id(1)
    lc = SSD_CHUNK
    hd = SSM_HEAD_DIM

    @pl.when(c == 0)
    def _():
        xbuf[0:8, :] = jnp.zeros((8, CONV_DIM), F32)
        state[...] = jnp.zeros(state.shape, F32)

    xbuf[8:8 + lc, :] = xbc_ref[...]
    cw = cw_ref[...]
    y = cb_ref[...]
    for i in range(SSM_CONV):
        y = y + cw[i:i + 1, :] * xbuf[8 - (SSM_CONV - 1) + i:8 - (SSM_CONV - 1) + i + lc, :]
    xbuf[0:8, :] = xbc_ref[lc - 8:lc, :]
    xc = _silu(y)
    x = xc[:, :D_SSM]
    dt = _softplus(dt_ref[...] + dtb_ref[...])
    da = dt * (-jnp.exp(alog_ref[...]))
    ii = lax.broadcasted_iota(jnp.int32, (lc, lc), 0)
    jj = lax.broadcasted_iota(jnp.int32, (lc, lc), 1)
    causal = jj <= ii
    acs = jnp.dot(causal.astype(F32), da, precision=HIGHEST, preferred_element_type=F32)
    acs_t = acs.T
    xdt = x * jnp.dot(dt, exp_ref[...], precision=HIGHEST, preferred_element_type=F32)
    xdt_t = xdt.T.astype(BF16)
    xdt_b = xdt.astype(BF16)
    ys = []
    for g in range(N_GROUPS_S):
        bm = xc[:, D_SSM + g * D_STATE:D_SSM + (g + 1) * D_STATE]
        cm = xc[:, D_SSM + (N_GROUPS_S + g) * D_STATE:D_SSM + (N_GROUPS_S + g + 1) * D_STATE]
        cmb = cm.astype(BF16)
        gmat = _dot_nt(cmb, bm.astype(BF16))
        for hh in range(N_HEADS_S // N_GROUPS_S):
            h = g * (N_HEADS_S // N_GROUPS_S) + hh
            a_col = acs[:, h:h + 1]
            a_row = acs_t[h:h + 1, :]
            a_last = acs[lc - 1:lc, h:h + 1]
            lmat = jnp.exp(jnp.where(causal, a_col - a_row, -jnp.inf))
            y_diag = _dot((gmat * lmat).astype(BF16), xdt_b[:, h * hd:(h + 1) * hd])
            st_prev = state[h]
            y_off = _dot_nt(cmb, st_prev.astype(BF16)) * jnp.exp(a_col)
            bdec = (bm * jnp.exp(a_last - a_col)).astype(BF16)
            state[h] = jnp.exp(a_last) * st_prev + _dot(xdt_t[h * hd:(h + 1) * hd, :], bdec)
            ys.append(y_diag + y_off)
    yy = jnp.concatenate(ys, axis=1) + x * dsk_ref[...]
    y_ref[...] = _rms(yy * _silu(z_ref[...]), g_ref[...]).astype(BF16)

    @pl.when(c == pl.num_programs(1) - 1)
    def _():
        st_ref[0] = state[...]


def ssd_prompt(proj, nb, seq, lw):
    nc = seq // SSD_CHUNK
    lc = SSD_CHUNK
    const = lambda b, c: (0, 0)
    return pl.pallas_call(
        _ssd_prompt_kernel,
        out_shape=(jax.ShapeDtypeStruct((nb * seq, D_SSM), BF16),
                   jax.ShapeDtypeStruct((nb, N_HEADS_S, SSM_HEAD_DIM, D_STATE), F32)),
        grid=(nb, nc),
        in_specs=[pl.BlockSpec((lc, CONV_DIM), lambda b, c: (b * nc + c, COL_XBC // CONV_DIM)),
                  pl.BlockSpec((lc, LANE), lambda b, c: (b * nc + c, COL_DT // LANE)),
                  pl.BlockSpec((lc, D_SSM), lambda b, c: (b * nc + c, COL_Z // D_SSM)),
                  pl.BlockSpec((SSM_CONV, CONV_DIM), const),
                  pl.BlockSpec((1, CONV_DIM), const),
                  pl.BlockSpec((1, LANE), const),
                  pl.BlockSpec((1, LANE), const),
                  pl.BlockSpec((1, D_SSM), const),
                  pl.BlockSpec((1, D_SSM), const),
                  pl.BlockSpec((LANE, D_SSM), const)],
        out_specs=(pl.BlockSpec((lc, D_SSM), lambda b, c: (b * nc + c, 0)),
                   pl.BlockSpec((1, N_HEADS_S, SSM_HEAD_DIM, D_STATE), lambda b, c: (b, 0, 0, 0))),
        scratch_shapes=[pltpu.VMEM((8 + lc, CONV_DIM), F32),
                        pltpu.VMEM((N_HEADS_S, SSM_HEAD_DIM, D_STATE), F32)],
        compiler_params=_cparams(("parallel", "arbitrary")),
        name="ssd_prompt",
    )(proj, proj, proj, lw['ssm_conv_w'], lw['ssm_conv_b'], lw['dt_bias'], lw['a_log'], lw['d_skip'],
      lw['g_ssm_out'], lw['head_expand'])


def _ssd_sample_prep_kernel(xbc_ref, p0_ref, p1_ref, p2_ref, dt_ref, cw_ref, cb_ref, dtb_ref, alog_ref, exp_ref,
                            x_ref, bc_ref, xdt_t_ref, dec_t_ref):
    cw = cw_ref[...]
    y = (cb_ref[...] + cw[0:1] * p0_ref[...] + cw[1:2] * p1_ref[...] + cw[2:3] * p2_ref[...]
         + cw[3:4] * xbc_ref[...])
    xc = _silu(y)
    x = xc[:, :D_SSM]
    dt = _softplus(dt_ref[...] + dtb_ref[...])
    dec = jnp.exp(dt * (-jnp.exp(alog_ref[...])))
    e = exp_ref[...]
    x_ref[...] = x
    bc_ref[...] = xc[:, D_SSM:]
    xdt_t_ref[...] = (x * jnp.dot(dt, e, precision=HIGHEST, preferred_element_type=F32)).T
    dec_t_ref[...] = jnp.dot(dec, e, precision=HIGHEST, preferred_element_type=F32).T


def ssd_sample_prep(proj, conv_state, lw):
    nb = proj.shape[0]
    const = lambda i: (0, 0)
    return pl.pallas_call(
        _ssd_sample_prep_kernel,
        out_shape=(jax.ShapeDtypeStruct((nb, D_SSM), F32),
                   jax.ShapeDtypeStruct((nb, 2 * N_GROUPS_S * D_STATE), F32),
                   jax.ShapeDtypeStruct((D_SSM, nb), F32),
                   jax.ShapeDtypeStruct((D_SSM, nb), F32)),
        grid=(1,),
        in_specs=[pl.BlockSpec((nb, CONV_DIM), lambda i: (0, COL_XBC // CONV_DIM)),
                  pl.BlockSpec((nb, CONV_DIM), lambda i: (0, 0)),
                  pl.BlockSpec((nb, CONV_DIM), lambda i: (0, 1)),
                  pl.BlockSpec((nb, CONV_DIM), lambda i: (0, 2)),
                  pl.BlockSpec((nb, LANE), lambda i: (0, COL_DT // LANE)),
                  pl.BlockSpec((SSM_CONV, CONV_DIM), const),
                  pl.BlockSpec((1, CONV_DIM), const),
                  pl.BlockSpec((1, LANE), const),
                  pl.BlockSpec((1, LANE), const),
                  pl.BlockSpec((LANE, D_SSM), const)],
        out_specs=(pl.BlockSpec((nb, D_SSM), const),
                   pl.BlockSpec((nb, 2 * N_GROUPS_S * D_STATE), const),
                   pl.BlockSpec((D_SSM, nb), const),
                   pl.BlockSpec((D_SSM, nb), const)),
        compiler_params=_cparams(("arbitrary",)),
        name="ssd_sample_prep",
    )(proj, conv_state, conv_state, conv_state, proj, lw['ssm_conv_w'], lw['ssm_conv_b'], lw['dt_bias'],
      lw['a_log'], lw['head_expand'])


SSD_STEP_BATCH = 8


def _ssd_sample_state_kernel(st_ref, xdt_t_ref, dec_t_ref, bc_ref, st_out, y_t_ref, *, bb):
    i0 = pl.program_id(0) * bb
    nb = xdt_t_ref.shape[1]
    lane = lax.broadcasted_iota(jnp.int32, (D_SSM, nb), 1)
    rows_per_group = D_SSM // N_GROUPS_S
    heads_per_group = N_HEADS_S // N_GROUPS_S

    @pl.when(i0 == 0)
    def _():
        y_t_ref[...] = jnp.zeros(y_t_ref.shape, F32)

    y_t = y_t_ref[...]
    for i in range(bb):
        b = i0 + i
        sel = lane == b
        xcol = jnp.sum(jnp.where(sel, xdt_t_ref[...], 0.0), axis=1, keepdims=True)
        dcol = jnp.sum(jnp.where(sel, dec_t_ref[...], 0.0), axis=1, keepdims=True)
        bc = bc_ref[pl.ds(b, 1), :]
        hs = st_ref[i].reshape(D_SSM, D_STATE)
        ycols = []
        for g in range(N_GROUPS_S):
            sl = slice(g * rows_per_group, (g + 1) * rows_per_group)
            brow = bc[:, g * D_STATE:(g + 1) * D_STATE]
            crow = bc[:, (N_GROUPS_S + g) * D_STATE:(N_GROUPS_S + g + 1) * D_STATE]
            new = hs[sl] * dcol[sl] + xcol[sl] * brow
            st_out[i, g * heads_per_group:(g + 1) * heads_per_group] = new.reshape(
                heads_per_group, SSM_HEAD_DIM, D_STATE)
            ycols.append(jnp.sum(new * crow, axis=1, keepdims=True))
        y_t = jnp.where(sel, jnp.concatenate(ycols, axis=0), y_t)
    y_t_ref[...] = y_t


def ssd_sample_state(li, state, xdt_t, dec_t, bc):
    nb = state.shape[1]
    bb = _tile(nb, SSD_STEP_BATCH)
    const = lambda b: (0, 0)
    st_spec = pl.BlockSpec((bb, N_HEADS_S, SSM_HEAD_DIM, D_STATE), lambda b: (b, 0, 0, 0))
    st_in_spec = pl.BlockSpec((None, bb, N_HEADS_S, SSM_HEAD_DIM, D_STATE), lambda b: (li, b, 0, 0, 0))
    return pl.pallas_call(
        functools.partial(_ssd_sample_state_kernel, bb=bb),
        out_shape=(jax.ShapeDtypeStruct(state.shape[1:], F32), jax.ShapeDtypeStruct((D_SSM, nb), F32)),
        grid=(nb // bb,),
        in_specs=[st_in_spec, pl.BlockSpec((D_SSM, nb), const), pl.BlockSpec((D_SSM, nb), const),
                  pl.BlockSpec((nb, 2 * N_GROUPS_S * D_STATE), const)],
        out_specs=(st_spec, pl.BlockSpec((D_SSM, nb), const)),
        compiler_params=_cparams(("arbitrary",)),
        name="ssd_sample_state",
    )(state, xdt_t, dec_t, bc)


def _sample_mix_kernel(y_t_ref, x_ref, z_ref, u_ref, v_ref, dsk_ref, gs_ref, gv_ref, wv_ref, bv_ref,
                       ssm_ref, cm_ref, vn_ref):
    yy = y_t_ref[...].T + x_ref[...] * dsk_ref[...]
    ssm_ref[...] = _rms(yy * _silu(z_ref[...]), gs_ref[...]).astype(BF16)
    vn = _rms(v_ref[...], gv_ref[...])
    vn_ref[...] = vn
    cm_ref[...] = (u_ref[...] * (vn * wv_ref[...] + bv_ref[...])).astype(BF16)


def sample_mix(y_t, x, proj, lw):
    nb = x.shape[0]
    const = lambda i: (0, 0)
    blk = lambda col: pl.BlockSpec((nb, D_SSM), lambda i, col=col: (0, col // D_SSM))
    vec = pl.BlockSpec((1, D_SSM), const)
    return pl.pallas_call(
        _sample_mix_kernel,
        out_shape=(jax.ShapeDtypeStruct((nb, D_SSM), BF16), jax.ShapeDtypeStruct((nb, D_CM), BF16),
                   jax.ShapeDtypeStruct((nb, D_CM), F32)),
        grid=(1,),
        in_specs=[pl.BlockSpec((D_SSM, nb), const), pl.BlockSpec((nb, D_SSM), const),
                  blk(COL_Z), blk(COL_U), blk(COL_V), vec, vec, vec, vec, vec],
        out_specs=(pl.BlockSpec((nb, D_SSM), const), pl.BlockSpec((nb, D_CM), const),
                   pl.BlockSpec((nb, D_CM), const)),
        compiler_params=_cparams(("arbitrary",)),
        name="sample_mix",
    )(y_t, x, proj, proj, proj, lw['d_skip'], lw['g_ssm_out'], lw['g_sgu'], lw['sgu_w0'], lw['sgu_b0'])


def _spatial_gate_kernel(u_ref, v_ref, g_ref, w_ref, bt_ref, o_ref, *, n_chunks):
    vn = _rms(v_ref[...], g_ref[...]).astype(BF16)
    ii = lax.broadcasted_iota(jnp.int32, (CHUNK, CHUNK), 0)
    jj = lax.broadcasted_iota(jnp.int32, (CHUNK, CHUNK), 1)
    bt = bt_ref[...]
    for g in range(N_CG):
        w = jnp.where(jj <= ii, w_ref[g], 0.0).astype(BF16)
        bcol = bt[:, g:g + 1]
        for c in range(n_chunks):
            rs = slice(c * CHUNK, (c + 1) * CHUNK)
            cs = slice(g * CG_DIM, (g + 1) * CG_DIM)
            mixed = _dot(w, vn[rs, cs]) + bcol
            o_ref[rs, cs] = (u_ref[rs, cs] * mixed).astype(BF16)


def spatial_gate(proj, lw, *, tm):
    r = proj.shape[0]
    const = lambda i: (0, 0)
    return pl.pallas_call(
        functools.partial(_spatial_gate_kernel, n_chunks=tm // CHUNK),
        out_shape=jax.ShapeDtypeStruct((r, D_CM), BF16),
        grid=(r // tm,),
        in_specs=[pl.BlockSpec((tm, D_CM), lambda i: (i, COL_U // D_CM)),
                  pl.BlockSpec((tm, D_CM), lambda i: (i, COL_V // D_CM)),
                  pl.BlockSpec((1, D_CM), const),
                  pl.BlockSpec((N_CG, CHUNK, CHUNK), lambda i: (0, 0, 0)),
                  pl.BlockSpec((CHUNK, N_CG), const)],
        out_specs=pl.BlockSpec((tm, D_CM), lambda i: (i, 0)),
        compiler_params=_cparams(("parallel",)),
        name="spatial_gate",
    )(proj, proj, lw['g_sgu'], lw['w_spatial'], lw['b_spatial_t'])


def _out_proj_kernel(x_ref, att_ref, ssm_ref, cm_ref, w_ref, o_ref):
    na = N_HEADS_A * V_DIM
    acc = _dot(att_ref[...], w_ref[0:na, :])
    acc = acc + _dot(ssm_ref[...], w_ref[na:na + D_SSM, :])
    acc = acc + _dot(cm_ref[...], w_ref[na + D_SSM:, :])
    o_ref[...] = x_ref[...] + acc


def out_proj(x, att, ssm, cm, w, li, *, tm, tn):
    r, d = x.shape
    w_mode = dict(pipeline_mode=pl.Buffered(1)) if tn == d else {}
    return pl.pallas_call(
        _out_proj_kernel,
        out_shape=jax.ShapeDtypeStruct((r, d), F32),
        grid=(r // tm, d // tn),
        in_specs=[pl.BlockSpec((tm, tn), lambda i, j: (i, j)),
                  pl.BlockSpec((tm, att.shape[1]), lambda i, j: (i, 0)),
                  pl.BlockSpec((tm, ssm.shape[1]), lambda i, j: (i, 0)),
                  pl.BlockSpec((tm, cm.shape[1]), lambda i, j: (i, 0)),
                  pl.BlockSpec((None, w.shape[1], tn), lambda i, j: (li, 0, j), **w_mode)],
        out_specs=pl.BlockSpec((tm, tn), lambda i, j: (i, j)),
        compiler_params=_cparams(("parallel", "parallel")),
        name="out_proj",
    )(x, att, ssm, cm, w)


FFN_HALO = 16


def _ffn_prompt_kernel(x_ref, xp_ref, g_ref, wg_ref, wu_ref, cw_ref, cb_ref, wd_ref, o_ref, h_ref, gbuf, acc_ref,
                       *, tm, seq):
    i = pl.program_id(0)
    j = pl.program_id(1)

    @pl.when(j == 0)
    def _():
        keep = ((i * tm) % seq != 0).astype(F32)
        h_ref[0:FFN_HALO, :] = (_rms(xp_ref[...], g_ref[...]) * keep).astype(BF16)
        h_ref[FFN_HALO:, :] = _rms(x_ref[...], g_ref[...]).astype(BF16)
        acc_ref[...] = jnp.zeros(acc_ref.shape, F32)

    gbuf[...] = _dot(h_ref[...], wg_ref[...])
    up = _dot(h_ref[FFN_HALO:, :], wu_ref[...])
    cw = cw_ref[...]
    y = cb_ref[...]
    for t in range(FFN_CONV):
        off = FFN_HALO - (FFN_CONV - 1) + t
        y = y + cw[t:t + 1, :] * gbuf[off:off + tm, :]
    acc_ref[...] += _dot((_silu(y) * up).astype(BF16), wd_ref[...])

    @pl.when(j == pl.num_programs(1) - 1)
    def _():
        o_ref[...] = x_ref[...] + acc_ref[...]


def ffn_prompt(x, seq, lw, *, tm, tf):
    r, d = x.shape
    li = lw['li']
    const = lambda i, j: (0, 0)
    halo_blocks = tm // FFN_HALO
    return pl.pallas_call(
        functools.partial(_ffn_prompt_kernel, tm=tm, seq=seq),
        out_shape=jax.ShapeDtypeStruct((r, d), F32),
        grid=(r // tm, D_FF // tf),
        in_specs=[pl.BlockSpec((tm, d), lambda i, j: (i, 0)),
                  pl.BlockSpec((FFN_HALO, d), lambda i, j: (jnp.maximum(i * halo_blocks - 1, 0), 0)),
                  pl.BlockSpec((1, d), const),
                  pl.BlockSpec((None, d, tf), lambda i, j: (li, 0, j)),
                  pl.BlockSpec((None, d, tf), lambda i, j: (li, 0, D_FF // tf + j)),
                  pl.BlockSpec((FFN_CONV, tf), lambda i, j: (0, j)),
                  pl.BlockSpec((1, tf), lambda i, j: (0, j)),
                  pl.BlockSpec((None, tf, d), lambda i, j: (li, j, 0))],
        out_specs=pl.BlockSpec((tm, d), lambda i, j: (i, 0)),
        scratch_shapes=[pltpu.VMEM((tm + FFN_HALO, d), BF16), pltpu.VMEM((tm + FFN_HALO, tf), F32),
                        pltpu.VMEM((tm, d), F32)],
        compiler_params=_cparams(("parallel", "arbitrary")),
        name="ffn_prompt",
    )(x, x, lw['g_ffn'], lw['w_gu'], lw['w_gu'], lw['ffn_conv_w'], lw['ffn_conv_b'], lw['w_down'])


def _ffn_sample_kernel(x_ref, p0_ref, p1_ref, g_ref, wg_ref, wu_ref, cw_ref, cb_ref, wd_ref, o_ref, gate_ref,
                       h_ref, acc_ref):
    j = pl.program_id(0)

    @pl.when(j == 0)
    def _():
        h_ref[...] = _rms(x_ref[...], g_ref[...]).astype(BF16)
        acc_ref[...] = jnp.zeros(acc_ref.shape, F32)

    gate = _dot(h_ref[...], wg_ref[...])
    up = _dot(h_ref[...], wu_ref[...])
    gate_ref[...] = gate
    cw = cw_ref[...]
    y = cb_ref[...] + cw[0:1] * p0_ref[...] + cw[1:2] * p1_ref[...] + cw[2:3] * gate
    acc_ref[...] += _dot((_silu(y) * up).astype(BF16), wd_ref[...])

    @pl.when(j == pl.num_programs(0) - 1)
    def _():
        o_ref[...] = x_ref[...] + acc_ref[...]


def ffn_sample(x, prev, lw, *, tf):
    nb, d = x.shape
    nf = D_FF // tf
    li = lw['li']
    return pl.pallas_call(
        _ffn_sample_kernel,
        out_shape=(jax.ShapeDtypeStruct((nb, d), F32), jax.ShapeDtypeStruct((nb, D_FF), F32)),
        grid=(nf,),
        in_specs=[pl.BlockSpec((nb, d), lambda j: (0, 0)),
                  pl.BlockSpec((nb, tf), lambda j: (0, j)),
                  pl.BlockSpec((nb, tf), lambda j: (0, nf + j)),
                  pl.BlockSpec((1, d), lambda j: (0, 0)),
                  pl.BlockSpec((None, d, tf), lambda j: (li, 0, j)),
                  pl.BlockSpec((None, d, tf), lambda j: (li, 0, nf + j)),
                  pl.BlockSpec((FFN_CONV, tf), lambda j: (0, j)),
                  pl.BlockSpec((1, tf), lambda j: (0, j)),
                  pl.BlockSpec((None, tf, d), lambda j: (li, j, 0))],
        out_specs=(pl.BlockSpec((nb, d), lambda j: (0, 0)), pl.BlockSpec((nb, tf), lambda j: (0, j))),
        scratch_shapes=[pltpu.VMEM((nb, d), BF16), pltpu.VMEM((nb, d), F32)],
        compiler_params=_cparams(("arbitrary",)),
        name="ffn_sample",
    )(x, prev, prev, lw['g_ffn'], lw['w_gu'], lw['w_gu'], lw['ffn_conv_w'], lw['ffn_conv_b'], lw['w_down'])


def _rope_tables(pos):
    inv = ROPE_THETA ** (-jnp.arange(0, ROPE_DIM, 2, dtype=F32) / ROPE_DIM)
    ang = pos.astype(F32)[:, None] * inv[None, :]
    cos, sin = jnp.cos(ang), jnp.sin(ang)
    zero = jnp.zeros_like(cos)
    return (jnp.concatenate([cos, cos, zero, zero], axis=1),
            jnp.concatenate([-sin, zero, zero, zero], axis=1),
            jnp.concatenate([zero, sin, zero, zero], axis=1))


def _pack_stacked(p):
    w_in = p['w_in']
    depth, d, _ = w_in.shape
    offs = np.cumsum([0, Q_LORA, KV_LORA, ROPE_DIM, D_SSM, CONV_DIM, N_HEADS_S, D_CM, D_CM])
    seg = lambda k: w_in[:, :, int(offs[k]):int(offs[k + 1])]
    zpad = lambda n: jnp.zeros((depth, d, n), w_in.dtype)
    w_in_p = jnp.concatenate([seg(0), seg(3), seg(6), seg(7), seg(4), seg(1),
                              seg(2), zpad(LANE - ROPE_DIM), seg(5), zpad(LANE - N_HEADS_S)], axis=2).astype(BF16)
    return {'w_in': w_in_p, 'w_out': p['w_out'].astype(BF16), 'w_gu': p['w_gate_up'].astype(BF16),
            'w_down': p['w_down'].astype(BF16)}


def _pack_layer(li, p, stacked):
    h = N_HEADS_A
    w_uq = jnp.pad(p['w_uq'][li], ((0, 0), (0, 0), (0, QK_PAD - QK_DIM))).reshape(Q_LORA, h * QK_PAD).astype(BF16)
    w_uk = p['w_uk'][li].reshape(KV_LORA, h * NOPE_DIM).astype(BF16)
    w_uv = p['w_uv'][li].reshape(KV_LORA, h * V_DIM).astype(BF16)
    pad_g = lambda g: jnp.pad(g, (0, QK_PAD - QK_DIM))[None, :]
    row = lambda v: v[None, :]
    lane_pad = lambda v: jnp.pad(v, (0, LANE - v.shape[0]))[None, :]
    head_expand = np.zeros((LANE, D_SSM), np.float32)
    head_expand[:N_HEADS_S] = np.kron(np.eye(N_HEADS_S, dtype=np.float32), np.ones((1, SSM_HEAD_DIM), np.float32))
    head_expand = jnp.asarray(head_expand)
    return {
        'li': li, **stacked,
        'g_mix': row(p['g_mix'][li]),
        'g_q_a': row(p['g_q_a'][li]), 'g_kv_a': row(p['g_kv_a'][li]),
        'w_uq': w_uq, 'w_uk': w_uk, 'w_uk_t': w_uk.T, 'w_uv': w_uv,
        'gq': pad_g(p['g_qk_q'][li]) * (QK_DIM ** -0.5),
        'gk': pad_g(p['g_qk_k'][li]),
        'ssm_conv_w': p['ssm_conv_w'][li], 'ssm_conv_b': row(p['ssm_conv_b'][li]),
        'dt_bias': lane_pad(p['ssm_dt_bias'][li]), 'a_log': lane_pad(p['ssm_a_log'][li]),
        'd_skip': row(jnp.repeat(p['ssm_d'][li], SSM_HEAD_DIM)),
        'g_ssm_out': row(p['g_ssm_out'][li]), 'head_expand': head_expand,
        'g_sgu': row(p['g_sgu'][li]), 'w_spatial': p['w_spatial'][li], 'b_spatial_t': p['b_spatial'][li].T,
        'sgu_w0': row(jnp.repeat(p['w_spatial'][li][:, 0, 0], CG_DIM)),
        'sgu_b0': row(jnp.repeat(p['b_spatial'][li][:, 0], CG_DIM)),
        'g_ffn': row(p['g_ffn'][li]),
        'ffn_conv_w': p['ffn_conv_w'][li], 'ffn_conv_b': row(p['ffn_conv_b'][li]),
    }


def _tile(n, pref):
    t = min(n, pref)
    assert n % t == 0, (n, pref)
    return t


def kernel(x_prompt, x_sample, cache_latent, cache_krope, state_ssm, state_conv, state_ffn_conv, page_table, g_mix, w_in, g_q_a, w_uq, g_kv_a, w_uk, w_uv, g_qk_q, g_qk_k, ssm_conv_w, ssm_conv_b, ssm_dt_bias, ssm_a_log, ssm_d, g_ssm_out, g_sgu, w_spatial, b_spatial, w_out, g_ffn, w_gate_up, ffn_conv_w, ffn_conv_b, w_down):
    params = dict(g_mix=g_mix, w_in=w_in, g_q_a=g_q_a, w_uq=w_uq, g_kv_a=g_kv_a, w_uk=w_uk, w_uv=w_uv,
                  g_qk_q=g_qk_q, g_qk_k=g_qk_k, ssm_conv_w=ssm_conv_w, ssm_conv_b=ssm_conv_b,
                  ssm_dt_bias=ssm_dt_bias, ssm_a_log=ssm_a_log, ssm_d=ssm_d, g_ssm_out=g_ssm_out, g_sgu=g_sgu,
                  w_spatial=w_spatial, b_spatial=b_spatial, w_out=w_out, g_ffn=g_ffn, w_gate_up=w_gate_up,
                  ffn_conv_w=ffn_conv_w, ffn_conv_b=ffn_conv_b, w_down=w_down)
    depth = w_in.shape[0]
    bp, seq, d = x_prompt.shape
    bd, dec_seq, _ = x_sample.shape
    assert dec_seq == 1 and d == D_MODEL and cache_latent.shape[2] == PAGE
    n_pages = page_table.shape[1]
    past_len = n_pages * PAGE
    rp = bp * seq

    tabs_p = _rope_tables(jnp.arange(seq, dtype=jnp.int32))
    tabs_s = _rope_tables(jnp.full((bd,), past_len, jnp.int32))

    tm_p = _tile(seq, 512)
    tq = _tile(seq, 1024)
    tk = tq
    pp = next(c for c in (64, 32, 16, 8, 4, 2) if n_pages % c == 0)
    cache_krope_t = jnp.swapaxes(cache_krope, 2, 3)
    tf = 512

    yp = x_prompt.reshape(rp, d)
    ys = x_sample.reshape(bd, d)
    outs = [[] for _ in range(11)]
    stacked = _pack_stacked(params)
    for li in range(depth):
        lw = _pack_layer(li, params, stacked)

        proj = norm_matmul(yp, lw['g_mix'], lw['w_in'], li, tm=tm_p, tn=IN_PAD)
        q, k, v, lat, kro = mla_prep(proj, bp, seq, lw, lw['gq'] * LOG2E, tabs_p, tm=_tile(seq, 256), q_dtype=BF16)
        att = flash_attention(q, k, v, tq=tq, tk=tk)
        ssm, ssm_state = ssd_prompt(proj, bp, seq, lw)
        cm = spatial_gate(proj, lw, tm=tm_p)
        y_mid = out_proj(yp, att.reshape(rp, -1), ssm, cm, lw['w_out'], li, tm=tm_p, tn=d)
        yp = ffn_prompt(y_mid, seq, lw, tm=tm_p, tf=tf)
        tail = y_mid.reshape(bp, seq, d)[:, seq - (FFN_CONV - 1):, :].reshape(bp * (FFN_CONV - 1), d)
        tail = jnp.pad(tail, ((0, 8 - tail.shape[0]), (0, 0)))
        gate_tail = norm_matmul(tail, lw['g_ffn'], lw['w_gu'], li, tm=8, tn=tf, n=D_FF)[:bp * (FFN_CONV - 1)]
        proj3 = proj.reshape(bp, seq, IN_PAD)
        outs[0].append(lat)
        outs[1].append(kro)
        outs[2].append(ssm_state)
        outs[3].append(proj3[:, seq - (SSM_CONV - 1):, COL_XBC:COL_XBC + CONV_DIM])
        outs[4].append(gate_tail.reshape(bp, FFN_CONV - 1, D_FF))

        proj_s = norm_matmul(ys, lw['g_mix'], lw['w_in'], li, tm=bd, tn=IN_PAD)
        q_s, k_s, _, lat_s, kro_s = mla_prep(proj_s, 1, bd, lw, lw['gq'], tabs_s, tm=bd, q_dtype=F32)
        qabs, qgr = dec_q(q_s, lw['gk'], lw['w_uk'])
        m, l, acc = dec_paged(li, page_table, cache_latent, cache_krope_t, lw['w_uk_t'],
                              qabs.transpose(1, 0, 2), qgr.transpose(1, 0, 2), pp=pp)
        att_s = dec_finish(m.transpose(1, 0, 2), l.transpose(1, 0, 2), acc.transpose(1, 0, 2), q_s, k_s,
                           lat_s[0], lw['w_uv'])
        conv_prev = state_conv[li]
        x_s, bc_s, xdt_t, dec_t = ssd_sample_prep(proj_s, conv_prev.reshape(bd, -1), lw)
        ssm_state_s, y_t = ssd_sample_state(li, state_ssm, xdt_t, dec_t, bc_s)
        ssm_s, cm_s, vn_s = sample_mix(y_t, x_s, proj_s, lw)
        ys_mid = out_proj(ys, att_s, ssm_s, cm_s, lw['w_out'], li, tm=bd, tn=d)
        ffn_prev = state_ffn_conv[li]
        ys, gate_s = ffn_sample(ys_mid, ffn_prev.reshape(bd, -1), lw, tf=tf)
        outs[5].append(lat_s.reshape(bd, 1, KV_LORA))
        outs[6].append(kro_s.reshape(bd, 1, ROPE_DIM))
        outs[7].append(ssm_state_s)
        outs[8].append(jnp.concatenate([conv_prev[:, 1:], proj_s[:, None, COL_XBC:COL_XBC + CONV_DIM]], axis=1))
        outs[9].append(jnp.concatenate([ffn_prev[:, 1:], gate_s[:, None, :]], axis=1))
        outs[10].append(vn_s.reshape(bd, 1, D_CM))

    stacked = [jnp.stack(o) for o in outs]
    return (yp.reshape(bp, seq, d), ys.reshape(bd, 1, d), *stacked)
```

```python
import functools

import numpy as np
import jax
import jax.numpy as jnp
from jax import lax
from jax.experimental import pallas as pl
from jax.experimental.pallas import tpu as pltpu

F32 = jnp.float32
BF16 = jnp.bfloat16
EPS = 1e-6
HIGHEST = lax.Precision.HIGHEST

D_MODEL = 2048
N_HEADS_A = 8
NOPE_DIM = 128
ROPE_DIM = 64
QK_DIM = NOPE_DIM + ROPE_DIM
QK_PAD = 256
V_DIM = 128
V_PAD = 256
Q_LORA = 512
KV_LORA = 256
ROPE_THETA = 10000.0
D_SSM = 512
SSM_HEAD_DIM = 64
N_HEADS_S = 8
N_GROUPS_S = 2
D_STATE = 128
SSM_CONV = 4
SSD_CHUNK = 128
CONV_DIM = 1024
D_CM = 512
CHUNK = 128
N_CG = 4
CG_DIM = 128
D_FF = 5632
FFN_CONV = 3
PAGE = 128

COL_CQ, COL_Z, COL_U, COL_V, COL_XBC, COL_CKV, COL_KR, COL_DT = 0, 512, 1024, 1536, 2048, 3072, 3328, 3456
IN_PAD = 3584

LANE = 128
VMEM_LIMIT = 56 * 1024 * 1024

NT_DIMS = (((1,), (1,)), ((), ()))


def _cparams(sem):
    return pltpu.CompilerParams(dimension_semantics=sem, vmem_limit_bytes=VMEM_LIMIT)


def _rms(x, g):
    return x * lax.rsqrt(jnp.mean(x * x, axis=-1, keepdims=True) + EPS) * g


def _silu(x):
    return x * (1.0 / (1.0 + jnp.exp(-x)))


def _softplus(x):
    return jnp.maximum(x, 0.0) + jnp.log(1.0 + jnp.exp(-jnp.abs(x)))


def _dot(a, b):
    return jnp.dot(a, b, preferred_element_type=F32)


def _dot_nt(a, b):
    return lax.dot_general(a, b, NT_DIMS, preferred_element_type=F32)


def _norm_matmul_kernel(x_ref, g_ref, w_ref, o_ref, h_ref):
    @pl.when(pl.program_id(1) == 0)
    def _():
        h_ref[...] = _rms(x_ref[...], g_ref[...]).astype(BF16)

    o_ref[...] = _dot(h_ref[...], w_ref[...])


def norm_matmul(x, g, w, li, *, tm, tn, n=None):
    r, k = x.shape
    n = w.shape[2] if n is None else n
    w_mode = dict(pipeline_mode=pl.Buffered(1)) if tn == n else {}
    return pl.pallas_call(
        _norm_matmul_kernel,
        out_shape=jax.ShapeDtypeStruct((r, n), F32),
        grid=(r // tm, n // tn),
        in_specs=[pl.BlockSpec((tm, k), lambda i, j: (i, 0)),
                  pl.BlockSpec((1, k), lambda i, j: (0, 0)),
                  pl.BlockSpec((None, k, tn), lambda i, j: (li, 0, j), **w_mode)],
        out_specs=pl.BlockSpec((tm, tn), lambda i, j: (i, j)),
        scratch_shapes=[pltpu.VMEM((tm, k), BF16)],
        compiler_params=_cparams(("parallel", "arbitrary")),
        name="norm_matmul",
    )(x, g, w)


def _mla_prep_kernel(cq_ref, ckv_ref, kr_ref, gqa_ref, gkva_ref, wuq_ref, wuk_ref, wuv_ref, gq_ref, gk_ref,
                     cos_ref, s1_ref, s2_ref, q_ref, k_ref, v_ref, lat_ref, kro_ref):
    cos = cos_ref[...]
    s1 = s1_ref[...]
    s2 = s2_ref[...]

    def rope(b):
        return b * cos + pltpu.roll(b, 96, 1) * s1 + pltpu.roll(b, 32, 1) * s2

    def head_norm(a, b, g):
        ss = jnp.sum(a * a + b * b, axis=-1, keepdims=True)
        inv = lax.rsqrt(ss * (1.0 / QK_DIM) + EPS)
        return a * inv * g[:, :NOPE_DIM], b * inv * g[:, NOPE_DIM:]

    cqn = _rms(cq_ref[...], gqa_ref[...]).astype(BF16)
    qf = _dot(cqn, wuq_ref[...])
    gq = gq_ref[...]
    for h in range(N_HEADS_A):
        a = qf[:, h * QK_PAD:h * QK_PAD + NOPE_DIM]
        b = rope(qf[:, h * QK_PAD + NOPE_DIM:(h + 1) * QK_PAD])
        a, b = head_norm(a, b, gq)
        q_ref[0, h, :, 0:NOPE_DIM] = a.astype(q_ref.dtype)
        q_ref[0, h, :, NOPE_DIM:QK_PAD] = b.astype(q_ref.dtype)

    ckvn = _rms(ckv_ref[...], gkva_ref[...])
    lat_ref[0] = ckvn
    cb = ckvn.astype(BF16)
    kn = _dot(cb, wuk_ref[...])
    vf = _dot(cb, wuv_ref[...])
    krr = rope(kr_ref[...])
    kro_ref[0] = krr[:, :ROPE_DIM]
    gk = gk_ref[...]
    lane = lax.broadcasted_iota(jnp.int32, (vf.shape[0], V_PAD - V_DIM), 1)
    ones_col = jnp.where(lane == 0, 1.0, 0.0).astype(BF16)
    for h in range(N_HEADS_A):
        a, b = head_norm(kn[:, h * NOPE_DIM:(h + 1) * NOPE_DIM], krr, gk)
        k_ref[0, h, :, 0:NOPE_DIM] = a.astype(BF16)
        k_ref[0, h, :, NOPE_DIM:QK_PAD] = b.astype(BF16)
        v_ref[0, h, :, 0:V_DIM] = vf[:, h * V_DIM:(h + 1) * V_DIM].astype(BF16)
        v_ref[0, h, :, V_DIM:V_PAD] = ones_col


def mla_prep(proj, nb, seq, lw, gq, tabs, *, tm, q_dtype):
    nt = seq // tm
    row = lambda b, t: (b * nt + t, 0)
    const = lambda b, t: (0, 0)
    tab = lambda b, t: (t, 0)
    h = N_HEADS_A
    return pl.pallas_call(
        _mla_prep_kernel,
        out_shape=(jax.ShapeDtypeStruct((nb, h, seq, QK_PAD), q_dtype),
                   jax.ShapeDtypeStruct((nb, h, seq, QK_PAD), BF16),
                   jax.ShapeDtypeStruct((nb, h, seq, V_PAD), BF16),
                   jax.ShapeDtypeStruct((nb, seq, KV_LORA), F32),
                   jax.ShapeDtypeStruct((nb, seq, ROPE_DIM), F32)),
        grid=(nb, nt),
        in_specs=[pl.BlockSpec((tm, Q_LORA), lambda b, t: (b * nt + t, COL_CQ // Q_LORA)),
                  pl.BlockSpec((tm, KV_LORA), lambda b, t: (b * nt + t, COL_CKV // KV_LORA)),
                  pl.BlockSpec((tm, LANE), lambda b, t: (b * nt + t, COL_KR // LANE)),
                  pl.BlockSpec((1, Q_LORA), const),
                  pl.BlockSpec((1, KV_LORA), const),
                  pl.BlockSpec((Q_LORA, h * QK_PAD), const),
                  pl.BlockSpec((KV_LORA, h * NOPE_DIM), const),
                  pl.BlockSpec((KV_LORA, h * V_DIM), const),
                  pl.BlockSpec((1, QK_PAD), const),
                  pl.BlockSpec((1, QK_PAD), const),
                  pl.BlockSpec((tm, LANE), tab),
                  pl.BlockSpec((tm, LANE), tab),
                  pl.BlockSpec((tm, LANE), tab)],
        out_specs=(pl.BlockSpec((1, h, tm, QK_PAD), lambda b, t: (b, 0, t, 0)),
                   pl.BlockSpec((1, h, tm, QK_PAD), lambda b, t: (b, 0, t, 0)),
                   pl.BlockSpec((1, h, tm, V_PAD), lambda b, t: (b, 0, t, 0)),
                   pl.BlockSpec((1, tm, KV_LORA), lambda b, t: (b, t, 0)),
                   pl.BlockSpec((1, tm, ROPE_DIM), lambda b, t: (b, t, 0))),
        compiler_params=_cparams(("parallel", "parallel")),
        name="mla_prep",
    )(proj, proj, proj, lw['g_q_a'], lw['g_kv_a'], lw['w_uq'], lw['w_uk'], lw['w_uv'], gq, lw['gk'],
      tabs[0], tabs[1], tabs[2])


FLASH_ROWS = 256
FLASH_CHUNK = 32
LOG2E = 1.4426950408889634


def _flash_kernel(qi_ref, ki_ref, q_ref, k_ref, v_ref, o_ref, m_ref, acc_ref, s_buf, p_buf, a_buf, *, tq, tk):
    p = pl.program_id(2)
    qi = qi_ref[p]
    ki = ki_ref[p]

    @pl.when(ki == 0)
    def _():
        m_ref[...] = jnp.full(m_ref.shape, -jnp.inf, F32)
        acc_ref[...] = jnp.zeros(acc_ref.shape, F32)

    n_sub = tq // FLASH_ROWS

    def scores(r, n_cols):
        rows = slice(r * FLASH_ROWS, (r + 1) * FLASH_ROWS)
        return _dot_nt(q_ref[0, 0, rows, :], k_ref[0, 0, 0:n_cols, :])

    def update(r, s, masked):
        rows = slice(r * FLASH_ROWS, (r + 1) * FLASH_ROWS)
        n_cols = s.shape[1]
        if masked:
            row = r * FLASH_ROWS + lax.broadcasted_iota(jnp.int32, s.shape, 0)
            col = lax.broadcasted_iota(jnp.int32, s.shape, 1)
            s = jnp.where(col <= row, s, -jnp.inf)
        s_buf[:, 0:n_cols] = s
        for c in range(FLASH_ROWS // FLASH_CHUNK):
            cr = slice(c * FLASH_CHUNK, (c + 1) * FLASH_CHUNK)
            gr = slice(r * FLASH_ROWS + c * FLASH_CHUNK, r * FLASH_ROWS + (c + 1) * FLASH_CHUNK)
            sc = s_buf[cr, 0:n_cols]
            m_prev = m_ref[gr, :]
            m_new = jnp.maximum(m_prev, jnp.max(sc, axis=-1, keepdims=True))
            a_buf[cr, :] = jnp.exp2(m_prev - m_new)
            p_buf[cr, 0:n_cols] = jnp.exp2(sc - m_new).astype(BF16)
            m_ref[gr, :] = m_new
        acc_ref[rows, :] = a_buf[...] * acc_ref[rows, :] + _dot(p_buf[:, 0:n_cols], v_ref[0, 0, 0:n_cols, :])

    def sweep(n_cols_of, masked):
        s_next = scores(0, n_cols_of(0))
        for r in range(n_sub):
            s = s_next
            if r + 1 < n_sub:
                s_next = scores(r + 1, n_cols_of(r + 1))
            update(r, s, masked)

    @pl.when(ki < qi)
    def _():
        sweep(lambda r: tk, False)

    @pl.when(ki == qi)
    def _():
        sweep(lambda r: (r + 1) * FLASH_ROWS, True)
        o_ref[0] = (acc_ref[:, 0:V_DIM] / acc_ref[:, V_DIM:V_DIM + 1]).astype(BF16)


def flash_attention(q, k, v, *, tq, tk):
    b, h, s, _ = q.shape
    assert tq == tk and tq % FLASH_ROWS == 0
    pairs = [(i, j) for i in range(s // tq) for j in range(i + 1)]
    qi_tab = jnp.asarray(np.array([p[0] for p in pairs], np.int32))
    ki_tab = jnp.asarray(np.array([p[1] for p in pairs], np.int32))
    return pl.pallas_call(
        functools.partial(_flash_kernel, tq=tq, tk=tk),
        out_shape=jax.ShapeDtypeStruct((b, s, h * V_DIM), BF16),
        grid_spec=pltpu.PrefetchScalarGridSpec(
            num_scalar_prefetch=2,
            grid=(b, h, len(pairs)),
            in_specs=[pl.BlockSpec((1, 1, tq, QK_PAD), lambda bi, hi, p, qt, kt: (bi, hi, qt[p], 0)),
                      pl.BlockSpec((1, 1, tk, QK_PAD), lambda bi, hi, p, qt, kt: (bi, hi, kt[p], 0)),
                      pl.BlockSpec((1, 1, tk, V_PAD), lambda bi, hi, p, qt, kt: (bi, hi, kt[p], 0))],
            out_specs=pl.BlockSpec((1, tq, V_DIM), lambda bi, hi, p, qt, kt: (bi, qt[p], hi)),
            scratch_shapes=[pltpu.VMEM((tq, 1), F32), pltpu.VMEM((tq, V_PAD), F32),
                            pltpu.VMEM((FLASH_ROWS, tk), F32), pltpu.VMEM((FLASH_ROWS, tk), BF16),
                            pltpu.VMEM((FLASH_ROWS, 1), F32)]),
        compiler_params=_cparams(("parallel", "parallel", "arbitrary")),
        name="flash_attention",
    )(qi_tab, ki_tab, q, k, v)


def _dec_q_kernel(q_ref, gk_ref, wuk_ref, qabs_ref, qgr_ref):
    gk = gk_ref[...]
    nb = q_ref.shape[2]
    for h in range(N_HEADS_A):
        qg = q_ref[0, h] * gk
        qn = qg[:, :NOPE_DIM].astype(BF16)
        qabs_ref[h] = _dot_nt(qn, wuk_ref[:, h * NOPE_DIM:(h + 1) * NOPE_DIM]).astype(BF16)
        qgr_ref[h] = qg[:, NOPE_DIM:QK_DIM].astype(BF16)
    for h in range(N_HEADS_A, 2 * N_HEADS_A):
        qabs_ref[h] = jnp.zeros((nb, KV_LORA), BF16)
        qgr_ref[h] = jnp.zeros((nb, ROPE_DIM), BF16)


def dec_q(q, gk, wuk):
    nb = q.shape[2]
    return pl.pallas_call(
        _dec_q_kernel,
        out_shape=(jax.ShapeDtypeStruct((2 * N_HEADS_A, nb, KV_LORA), BF16),
                   jax.ShapeDtypeStruct((2 * N_HEADS_A, nb, ROPE_DIM), BF16)),
        compiler_params=pltpu.CompilerParams(vmem_limit_bytes=VMEM_LIMIT),
        name="dec_q",
    )(q, gk, wuk)


DEC_SLOTS = 3


def _dec_paged_kernel(pt_ref, lat_hbm, krt_hbm, wukt_ref, qabs_ref, qgr_ref, m_out, l_out, acc_out,
                      latbuf, krbuf, sem, lhs_ref, cbuf, m_s, l_s, acc_s, *, li, pp):
    j = pl.program_id(1)
    nj = pl.num_programs(1)
    n_steps = pl.num_programs(0) * nj
    step = pl.program_id(0) * nj + j
    slot = step % DEC_SLOTS
    ahead = [(lax.rem(step + a, n_steps), lax.rem(step + a, DEC_SLOTS)) for a in (1, 2)]
    nh = N_HEADS_A
    n_k = nh * NOPE_DIM

    def page_copies(st, sl):
        row = st // nj
        col = (st % nj) * pp
        copies = []
        for i in range(pp):
            pid = pt_ref[row, col + i]
            copies.append(pltpu.make_async_copy(lat_hbm.at[li, pid], latbuf.at[sl, i], sem.at[0, sl]))
            copies.append(pltpu.make_async_copy(krt_hbm.at[li, pid], krbuf.at[sl, i], sem.at[1, sl]))
        return copies

    @pl.when(step == 0)
    def _():
        for c in page_copies(step, slot) + page_copies(*ahead[0]):
            c.start()

    @pl.when(j == 0)
    def _():
        lhs_ref[0:n_k, :] = wukt_ref[...]
        lhs_ref[n_k:n_k + 2 * nh, :] = qabs_ref[0]
        m_s[...] = jnp.full(m_s.shape, -jnp.inf, F32)
        l_s[...] = jnp.zeros(l_s.shape, F32)
        acc_s[...] = jnp.zeros(acc_s.shape, F32)

    for c in page_copies(step, slot):
        c.wait()

    qgr = qgr_ref[0]
    parts = []
    for t in range(pp // 2):
        cb = jnp.concatenate([latbuf[slot, 2 * t], latbuf[slot, 2 * t + 1]], axis=0).astype(BF16)
        cbuf[t * 2 * PAGE:(t + 1) * 2 * PAGE, :] = cb
        kt = _dot_nt(lhs_ref[...], cb)
        k3 = kt[:n_k].reshape(nh, NOPE_DIM, 2 * PAGE)
        ssq = jnp.sum(k3 * k3, axis=1)
        krt = jnp.concatenate([krbuf[slot, 2 * t], krbuf[slot, 2 * t + 1]], axis=1)
        ssq = ssq + jnp.sum(krt * krt, axis=0, keepdims=True)
        num = kt[n_k:n_k + nh] + _dot(qgr, krt.astype(BF16))[:nh]
        parts.append(num * lax.rsqrt(ssq * (1.0 / QK_DIM) + EPS))
    s = jnp.concatenate(parts, axis=1)
    m_prev = m_s[...]
    m_new = jnp.maximum(m_prev, jnp.max(s, axis=-1, keepdims=True))
    alpha = jnp.exp(m_prev - m_new)
    p = jnp.exp(s - m_new)
    l_s[...] = alpha * l_s[...] + jnp.sum(p, axis=-1, keepdims=True)
    pb = jnp.concatenate([p, jnp.zeros_like(p)], axis=0).astype(BF16)
    for c in page_copies(*ahead[1]):
        c.start()
    acc_s[...] = alpha * acc_s[...] + _dot(pb, cbuf[...])[:nh]
    m_s[...] = m_new

    @pl.when(j == pl.num_programs(1) - 1)
    def _():
        m_out[0] = jnp.broadcast_to(m_s[...], (nh, LANE))
        l_out[0] = jnp.broadcast_to(l_s[...], (nh, LANE))
        acc_out[0] = acc_s[...]

    @pl.when(step == n_steps - 1)
    def _():
        for c in page_copies(*ahead[0]) + page_copies(*ahead[1]):
            c.wait()


def dec_paged(li, page_table, cache_lat, cache_kr_t, wukt, qabs, qgr, *, pp):
    nb, n_pages = page_table.shape
    nh = N_HEADS_A
    return pl.pallas_call(
        functools.partial(_dec_paged_kernel, li=li, pp=pp),
        out_shape=(jax.ShapeDtypeStruct((nb, nh, LANE), F32),
                   jax.ShapeDtypeStruct((nb, nh, LANE), F32),
                   jax.ShapeDtypeStruct((nb, nh, KV_LORA), F32)),
        grid_spec=pltpu.PrefetchScalarGridSpec(
            num_scalar_prefetch=1,
            grid=(nb, n_pages // pp),
            in_specs=[
                pl.BlockSpec(memory_space=pl.ANY),
                pl.BlockSpec(memory_space=pl.ANY),
                pl.BlockSpec((nh * NOPE_DIM, KV_LORA), lambda b, j, pt: (0, 0)),
                pl.BlockSpec((1, 2 * nh, KV_LORA), lambda b, j, pt: (b, 0, 0)),
                pl.BlockSpec((1, 2 * nh, ROPE_DIM), lambda b, j, pt: (b, 0, 0))],
            out_specs=(pl.BlockSpec((1, nh, LANE), lambda b, j, pt: (b, 0, 0)),
                       pl.BlockSpec((1, nh, LANE), lambda b, j, pt: (b, 0, 0)),
                       pl.BlockSpec((1, nh, KV_LORA), lambda b, j, pt: (b, 0, 0))),
            scratch_shapes=[pltpu.VMEM((DEC_SLOTS, pp, PAGE, KV_LORA), F32),
                            pltpu.VMEM((DEC_SLOTS, pp, ROPE_DIM, PAGE), F32),
                            pltpu.SemaphoreType.DMA((2, DEC_SLOTS)),
                            pltpu.VMEM((nh * NOPE_DIM + 2 * nh, KV_LORA), BF16),
                            pltpu.VMEM((pp * PAGE, KV_LORA), BF16),
                            pltpu.VMEM((nh, 1), F32), pltpu.VMEM((nh, 1), F32), pltpu.VMEM((nh, KV_LORA), F32)]),
        compiler_params=_cparams(("arbitrary", "arbitrary")),
        name="dec_paged",
    )(page_table, cache_lat, cache_kr_t, wukt, qabs, qgr)


def _dec_finish_kernel(m_ref, l_ref, acc_ref, q_ref, k_ref, lat_ref, wuv_ref, o_ref):
    cn = lat_ref[...].astype(BF16).astype(F32)
    for h in range(N_HEADS_A):
        qh = q_ref[0, h].astype(BF16).astype(F32)
        s_new = jnp.sum(qh * k_ref[0, h].astype(F32), axis=-1, keepdims=True)
        m_h = m_ref[h][:, 0:1]
        l_h = l_ref[h][:, 0:1]
        m_tot = jnp.maximum(m_h, s_new)
        a = jnp.exp(m_h - m_tot)
        pn = jnp.exp(s_new - m_tot)
        lat = (acc_ref[h] * a + pn * cn) / (l_h * a + pn)
        o_ref[:, h * V_DIM:(h + 1) * V_DIM] = _dot(lat.astype(BF16), wuv_ref[:, h * V_DIM:(h + 1) * V_DIM]).astype(BF16)


def dec_finish(m, l, acc, q, k, lat, wuv):
    nb = lat.shape[0]
    return pl.pallas_call(
        _dec_finish_kernel,
        out_shape=jax.ShapeDtypeStruct((nb, N_HEADS_A * V_DIM), BF16),
        compiler_params=pltpu.CompilerParams(vmem_limit_bytes=VMEM_LIMIT),
        name="dec_finish",
    )(m, l, acc, q, k, lat, wuv)


def _ssd_prompt_kernel(xbc_ref, dt_ref, z_ref, cw_ref, cb_ref, dtb_ref, alog_ref, dsk_ref, g_ref, exp_ref,
                       y_ref, st_ref, xbuf, state):
    c = pl.program_id(1)
    lc = SSD_CHUNK
    hd = SSM_HEAD_DIM

    @pl.when(c == 0)
    def _():
        xbuf[0:8, :] = jnp.zeros((8, CONV_DIM), F32)
        state[...] = jnp.zeros(state.shape, F32)

    xbuf[8:8 + lc, :] = xbc_ref[...]
    cw = cw_ref[...]
    y = cb_ref[...]
    for i in range(SSM_CONV):
        y = y + cw[i:i + 1, :] * xbuf[8 - (SSM_CONV - 1) + i:8 - (SSM_CONV - 1) + i + lc, :]
    xbuf[0:8, :] = xbc_ref[lc - 8:lc, :]
    xc = _silu(y)
    x = xc[:, :D_SSM]
    dt = _softplus(dt_ref[...] + dtb_ref[...])
    da = dt * (-jnp.exp(alog_ref[...]))
    ii = lax.broadcasted_iota(jnp.int32, (lc, lc), 0)
    jj = lax.broadcasted_iota(jnp.int32, (lc, lc), 1)
    causal = jj <= ii
    acs = jnp.dot(causal.astype(F32), da, precision=HIGHEST, preferred_element_type=F32)
    acs_t = acs.T
    xdt = x * jnp.dot(dt, exp_ref[...], precision=HIGHEST, preferred_element_type=F32)
    xdt_t = xdt.T.astype(BF16)
    xdt_b = xdt.astype(BF16)
    ys = []
    for g in range(N_GROUPS_S):
        bm = xc[:, D_SSM + g * D_STATE:D_SSM + (g + 1) * D_STATE]
        cm = xc[:, D_SSM + (N_GROUPS_S + g) * D_STATE:D_SSM + (N_GROUPS_S + g + 1) * D_STATE]
        cmb = cm.astype(BF16)
        gmat = _dot_nt(cmb, bm.astype(BF16))
        for hh in range(N_HEADS_S // N_GROUPS_S):
            h = g * (N_HEADS_S // N_GROUPS_S) + hh
            a_col = acs[:, h:h + 1]
            a_row = acs_t[h:h + 1, :]
            a_last = acs[lc - 1:lc, h:h + 1]
            lmat = jnp.exp(jnp.where(causal, a_col - a_row, -jnp.inf))
            y_diag = _dot((gmat * lmat).astype(BF16), xdt_b[:, h * hd:(h + 1) * hd])
            st_prev = state[h]
            y_off = _dot_nt(cmb, st_prev.astype(BF16)) * jnp.exp(a_col)
            bdec = (bm * jnp.exp(a_last - a_col)).astype(BF16)
            state[h] = jnp.exp(a_last) * st_prev + _dot(xdt_t[h * hd:(h + 1) * hd, :], bdec)
            ys.append(y_diag + y_off)
    yy = jnp.concatenate(ys, axis=1) + x * dsk_ref[...]
    y_ref[...] = _rms(yy * _silu(z_ref[...]), g_ref[...]).astype(BF16)

    @pl.when(c == pl.num_programs(1) - 1)
    def _():
        st_ref[0] = state[...]


def ssd_prompt(proj, nb, seq, lw):
    nc = seq // SSD_CHUNK
    lc = SSD_CHUNK
    const = lambda b, c: (0, 0)
    return pl.pallas_call(
        _ssd_prompt_kernel,
        out_shape=(jax.ShapeDtypeStruct((nb * seq, D_SSM), BF16),
                   jax.ShapeDtypeStruct((nb, N_HEADS_S, SSM_HEAD_DIM, D_STATE), F32)),
        grid=(nb, nc),
        in_specs=[pl.BlockSpec((lc, CONV_DIM), lambda b, c: (b * nc + c, COL_XBC // CONV_DIM)),
                  pl.BlockSpec((lc, LANE), lambda b, c: (b * nc + c, COL_DT // LANE)),
                  pl.BlockSpec((lc, D_SSM), lambda b, c: (b * nc + c, COL_Z // D_SSM)),
                  pl.BlockSpec((SSM_CONV, CONV_DIM), const),
                  pl.BlockSpec((1, CONV_DIM), const),
                  pl.BlockSpec((1, LANE), const),
                  pl.BlockSpec((1, LANE), const),
                  pl.BlockSpec((1, D_SSM), const),
                  pl.BlockSpec((1, D_SSM), const),
                  pl.BlockSpec((LANE, D_SSM), const)],
        out_specs=(pl.BlockSpec((lc, D_SSM), lambda b, c: (b * nc + c, 0)),
                   pl.BlockSpec((1, N_HEADS_S, SSM_HEAD_DIM, D_STATE), lambda b, c: (b, 0, 0, 0))),
        scratch_shapes=[pltpu.VMEM((8 + lc, CONV_DIM), F32),
                        pltpu.VMEM((N_HEADS_S, SSM_HEAD_DIM, D_STATE), F32)],
        compiler_params=_cparams(("parallel", "arbitrary")),
        name="ssd_prompt",
    )(proj, proj, proj, lw['ssm_conv_w'], lw['ssm_conv_b'], lw['dt_bias'], lw['a_log'], lw['d_skip'],
      lw['g_ssm_out'], lw['head_expand'])


def _ssd_sample_prep_kernel(xbc_ref, p0_ref, p1_ref, p2_ref, dt_ref, cw_ref, cb_ref, dtb_ref, alog_ref, exp_ref,
                            x_ref, bc_ref, xdt_t_ref, dec_t_ref):
    cw = cw_ref[...]
    y = (cb_ref[...] + cw[0:1] * p0_ref[...] + cw[1:2] * p1_ref[...] + cw[2:3] * p2_ref[...]
         + cw[3:4] * xbc_ref[...])
    xc = _silu(y)
    x = xc[:, :D_SSM]
    dt = _softplus(dt_ref[...] + dtb_ref[...])
    dec = jnp.exp(dt * (-jnp.exp(alog_ref[...])))
    e = exp_ref[...]
    x_ref[...] = x
    bc_ref[...] = xc[:, D_SSM:]
    xdt_t_ref[...] = (x * jnp.dot(dt, e, precision=HIGHEST, preferred_element_type=F32)).T
    dec_t_ref[...] = jnp.dot(dec, e, precision=HIGHEST, preferred_element_type=F32).T


def ssd_sample_prep(proj, conv_state, lw):
    nb = proj.shape[0]
    const = lambda i: (0, 0)
    return pl.pallas_call(
        _ssd_sample_prep_kernel,
        out_shape=(jax.ShapeDtypeStruct((nb, D_SSM), F32),
                   jax.ShapeDtypeStruct((nb, 2 * N_GROUPS_S * D_STATE), F32),
                   jax.ShapeDtypeStruct((D_SSM, nb), F32),
                   jax.ShapeDtypeStruct((D_SSM, nb), F32)),
        grid=(1,),
        in_specs=[pl.BlockSpec((nb, CONV_DIM), lambda i: (0, COL_XBC // CONV_DIM)),
                  pl.BlockSpec((nb, CONV_DIM), lambda i: (0, 0)),
                  pl.BlockSpec((nb, CONV_DIM), lambda i: (0, 1)),
                  pl.BlockSpec((nb, CONV_DIM), lambda i: (0, 2)),
                  pl.BlockSpec((nb, LANE), lambda i: (0, COL_DT // LANE)),
                  pl.BlockSpec((SSM_CONV, CONV_DIM), const),
                  pl.BlockSpec((1, CONV_DIM), const),
                  pl.BlockSpec((1, LANE), const),
                  pl.BlockSpec((1, LANE), const),
                  pl.BlockSpec((LANE, D_SSM), const)],
        out_specs=(pl.BlockSpec((nb, D_SSM), const),
                   pl.BlockSpec((nb, 2 * N_GROUPS_S * D_STATE), const),
                   pl.BlockSpec((D_SSM, nb), const),
                   pl.BlockSpec((D_SSM, nb), const)),
        compiler_params=_cparams(("arbitrary",)),
        name="ssd_sample_prep",
    )(proj, conv_state, conv_state, conv_state, proj, lw['ssm_conv_w'], lw['ssm_conv_b'], lw['dt_bias'],
      lw['a_log'], lw['head_expand'])


SSD_STEP_BATCH = 8


def _ssd_sample_state_kernel(st_ref, xdt_t_ref, dec_t_ref, bc_ref, st_out, y_t_ref, *, bb):
    i0 = pl.program_id(0) * bb
    nb = xdt_t_ref.shape[1]
    lane = lax.broadcasted_iota(jnp.int32, (D_SSM, nb), 1)
    rows_per_group = D_SSM // N_GROUPS_S
    heads_per_group = N_HEADS_S // N_GROUPS_S

    @pl.when(i0 == 0)
    def _():
        y_t_ref[...] = jnp.zeros(y_t_ref.shape, F32)

    y_t = y_t_ref[...]
    for i in range(bb):
        b = i0 + i
        sel = lane == b
        xcol = jnp.sum(jnp.where(sel, xdt_t_ref[...], 0.0), axis=1, keepdims=True)
        dcol = jnp.sum(jnp.where(sel, dec_t_ref[...], 0.0), axis=1, keepdims=True)
        bc = bc_ref[pl.ds(b, 1), :]
        hs = st_ref[i].reshape(D_SSM, D_STATE)
        ycols = []
        for g in range(N_GROUPS_S):
            sl = slice(g * rows_per_group, (g + 1) * rows_per_group)
            brow = bc[:, g * D_STATE:(g + 1) * D_STATE]
            crow = bc[:, (N_GROUPS_S + g) * D_STATE:(N_GROUPS_S + g + 1) * D_STATE]
            new = hs[sl] * dcol[sl] + xcol[sl] * brow
            st_out[i, g * heads_per_group:(g + 1) * heads_per_group] = new.reshape(
                heads_per_group, SSM_HEAD_DIM, D_STATE)
            ycols.append(jnp.sum(new * crow, axis=1, keepdims=True))
        y_t = jnp.where(sel, jnp.concatenate(ycols, axis=0), y_t)
    y_t_ref[...] = y_t


def ssd_sample_state(li, state, xdt_t, dec_t, bc):
    nb = state.shape[1]
    bb = _tile(nb, SSD_STEP_BATCH)
    const = lambda b: (0, 0)
    st_spec = pl.BlockSpec((bb, N_HEADS_S, SSM_HEAD_DIM, D_STATE), lambda b: (b, 0, 0, 0))
    st_in_spec = pl.BlockSpec((None, bb, N_HEADS_S, SSM_HEAD_DIM, D_STATE), lambda b: (li, b, 0, 0, 0))
    return pl.pallas_call(
        functools.partial(_ssd_sample_state_kernel, bb=bb),
        out_shape=(jax.ShapeDtypeStruct(state.shape[1:], F32), jax.ShapeDtypeStruct((D_SSM, nb), F32)),
        grid=(nb // bb,),
        in_specs=[st_in_spec, pl.BlockSpec((D_SSM, nb), const), pl.BlockSpec((D_SSM, nb), const),
                  pl.BlockSpec((nb, 2 * N_GROUPS_S * D_STATE), const)],
        out_specs=(st_spec, pl.BlockSpec((D_SSM, nb), const)),
        compiler_params=_cparams(("arbitrary",)),
        name="ssd_sample_state",
    )(state, xdt_t, dec_t, bc)


def _sample_mix_kernel(y_t_ref, x_ref, z_ref, u_ref, v_ref, dsk_ref, gs_ref, gv_ref, wv_ref, bv_ref,
                       ssm_ref, cm_ref, vn_ref):
    yy = y_t_ref[...].T + x_ref[...] * dsk_ref[...]
    ssm_ref[...] = _rms(yy * _silu(z_ref[...]), gs_ref[...]).astype(BF16)
    vn = _rms(v_ref[...], gv_ref[...])
    vn_ref[...] = vn
    cm_ref[...] = (u_ref[...] * (vn * wv_ref[...] + bv_ref[...])).astype(BF16)


def sample_mix(y_t, x, proj, lw):
    nb = x.shape[0]
    const = lambda i: (0, 0)
    blk = lambda col: pl.BlockSpec((nb, D_SSM), lambda i, col=col: (0, col // D_SSM))
    vec = pl.BlockSpec((1, D_SSM), const)
    return pl.pallas_call(
        _sample_mix_kernel,
        out_shape=(jax.ShapeDtypeStruct((nb, D_SSM), BF16), jax.ShapeDtypeStruct((nb, D_CM), BF16),
                   jax.ShapeDtypeStruct((nb, D_CM), F32)),
        grid=(1,),
        in_specs=[pl.BlockSpec((D_SSM, nb), const), pl.BlockSpec((nb, D_SSM), const),
                  blk(COL_Z), blk(COL_U), blk(COL_V), vec, vec, vec, vec, vec],
        out_specs=(pl.BlockSpec((nb, D_SSM), const), pl.BlockSpec((nb, D_CM), const),
                   pl.BlockSpec((nb, D_CM), const)),
        compiler_params=_cparams(("arbitrary",)),
        name="sample_mix",
    )(y_t, x, proj, proj, proj, lw['d_skip'], lw['g_ssm_out'], lw['g_sgu'], lw['sgu_w0'], lw['sgu_b0'])


def _spatial_gate_kernel(u_ref, v_ref, g_ref, w_ref, bt_ref, o_ref, *, n_chunks):
    vn = _rms(v_ref[...], g_ref[...]).astype(BF16)
    ii = lax.broadcasted_iota(jnp.int32, (CHUNK, CHUNK), 0)
    jj = lax.broadcasted_iota(jnp.int32, (CHUNK, CHUNK), 1)
    bt = bt_ref[...]
    for g in range(N_CG):
        w = jnp.where(jj <= ii, w_ref[g], 0.0).astype(BF16)
        bcol = bt[:, g:g + 1]
        for c in range(n_chunks):
            rs = slice(c * CHUNK, (c + 1) * CHUNK)
            cs = slice(g * CG_DIM, (g + 1) * CG_DIM)
            mixed = _dot(w, vn[rs, cs]) + bcol
            o_ref[rs, cs] = (u_ref[rs, cs] * mixed).astype(BF16)


def spatial_gate(proj, lw, *, tm):
    r = proj.shape[0]
    const = lambda i: (0, 0)
    return pl.pallas_call(
        functools.partial(_spatial_gate_kernel, n_chunks=tm // CHUNK),
        out_shape=jax.ShapeDtypeStruct((r, D_CM), BF16),
        grid=(r // tm,),
        in_specs=[pl.BlockSpec((tm, D_CM), lambda i: (i, COL_U // D_CM)),
                  pl.BlockSpec((tm, D_CM), lambda i: (i, COL_V // D_CM)),
                  pl.BlockSpec((1, D_CM), const),
                  pl.BlockSpec((N_CG, CHUNK, CHUNK), lambda i: (0, 0, 0)),
                  pl.BlockSpec((CHUNK, N_CG), const)],
        out_specs=pl.BlockSpec((tm, D_CM), lambda i: (i, 0)),
        compiler_params=_cparams(("parallel",)),
        name="spatial_gate",
    )(proj, proj, lw['g_sgu'], lw['w_spatial'], lw['b_spatial_t'])


def _out_proj_kernel(x_ref, att_ref, ssm_ref, cm_ref, w_ref, o_ref):
    na = N_HEADS_A * V_DIM
    acc = _dot(att_ref[...], w_ref[0:na, :])
    acc = acc + _dot(ssm_ref[...], w_ref[na:na + D_SSM, :])
    acc = acc + _dot(cm_ref[...], w_ref[na + D_SSM:, :])
    o_ref[...] = x_ref[...] + acc


def out_proj(x, att, ssm, cm, w, li, *, tm, tn):
    r, d = x.shape
    w_mode = dict(pipeline_mode=pl.Buffered(1)) if tn == d else {}
    return pl.pallas_call(
        _out_proj_kernel,
        out_shape=jax.ShapeDtypeStruct((r, d), F32),
        grid=(r // tm, d // tn),
        in_specs=[pl.BlockSpec((tm, tn), lambda i, j: (i, j)),
                  pl.BlockSpec((tm, att.shape[1]), lambda i, j: (i, 0)),
                  pl.BlockSpec((tm, ssm.shape[1]), lambda i, j: (i, 0)),
                  pl.BlockSpec((tm, cm.shape[1]), lambda i, j: (i, 0)),
                  pl.BlockSpec((None, w.shape[1], tn), lambda i, j: (li, 0, j), **w_mode)],
        out_specs=pl.BlockSpec((tm, tn), lambda i, j: (i, j)),
        compiler_params=_cparams(("parallel", "parallel")),
        name="out_proj",
    )(x, att, ssm, cm, w)


FFN_HALO = 16


def _ffn_prompt_kernel(x_ref, xp_ref, g_ref, wg_ref, wu_ref, cw_ref, cb_ref, wd_ref, o_ref, h_ref, gbuf, acc_ref,
                       *, tm, seq):
    i = pl.program_id(0)
    j = pl.program_id(1)

    @pl.when(j == 0)
    def _():
        keep = ((i * tm) % seq != 0).astype(F32)
        h_ref[0:FFN_HALO, :] = (_rms(xp_ref[...], g_ref[...]) * keep).astype(BF16)
        h_ref[FFN_HALO:, :] = _rms(x_ref[...], g_ref[...]).astype(BF16)
        acc_ref[...] = jnp.zeros(acc_ref.shape, F32)

    gbuf[...] = _dot(h_ref[...], wg_ref[...])
    up = _dot(h_ref[FFN_HALO:, :], wu_ref[...])
    cw = cw_ref[...]
    y = cb_ref[...]
    for t in range(FFN_CONV):
        off = FFN_HALO - (FFN_CONV - 1) + t
        y = y + cw[t:t + 1, :] * gbuf[off:off + tm, :]
    acc_ref[...] += _dot((_silu(y) * up).astype(BF16), wd_ref[...])

    @pl.when(j == pl.num_programs(1) - 1)
    def _():
        o_ref[...] = x_ref[...] + acc_ref[...]


def ffn_prompt(x, seq, lw, *, tm, tf):
    r, d = x.shape
    li = lw['li']
    const = lambda i, j: (0, 0)
    halo_blocks = tm // FFN_HALO
    return pl.pallas_call(
        functools.partial(_ffn_prompt_kernel, tm=tm, seq=seq),
        out_shape=jax.ShapeDtypeStruct((r, d), F32),
        grid=(r // tm, D_FF // tf),
        in_specs=[pl.BlockSpec((tm, d), lambda i, j: (i, 0)),
                  pl.BlockSpec((FFN_HALO, d), lambda i, j: (jnp.maximum(i * halo_blocks - 1, 0), 0)),
                  pl.BlockSpec((1, d), const),
                  pl.BlockSpec((None, d, tf), lambda i, j: (li, 0, j)),
                  pl.BlockSpec((None, d, tf), lambda i, j: (li, 0, D_FF // tf + j)),
                  pl.BlockSpec((FFN_CONV, tf), lambda i, j: (0, j)),
                  pl.BlockSpec((1, tf), lambda i, j: (0, j)),
                  pl.BlockSpec((None, tf, d), lambda i, j: (li, j, 0))],
        out_specs=pl.BlockSpec((tm, d), lambda i, j: (i, 0)),
        scratch_shapes=[pltpu.VMEM((tm + FFN_HALO, d), BF16), pltpu.VMEM((tm + FFN_HALO, tf), F32),
                        pltpu.VMEM((tm, d), F32)],
        compiler_params=_cparams(("parallel", "arbitrary")),
        name="ffn_prompt",
    )(x, x, lw['g_ffn'], lw['w_gu'], lw['w_gu'], lw['ffn_conv_w'], lw['ffn_conv_b'], lw['w_down'])


def _ffn_sample_kernel(x_ref, p0_ref, p1_ref, g_ref, wg_ref, wu_ref, cw_ref, cb_ref, wd_ref, o_ref, gate_ref,
                       h_ref, acc_ref):
    j = pl.program_id(0)

    @pl.when(j == 0)
    def _():
        h_ref[...] = _rms(x_ref[...], g_ref[...]).astype(BF16)
        acc_ref[...] = jnp.zeros(acc_ref.shape, F32)

    gate = _dot(h_ref[...], wg_ref[...])
    up = _dot(h_ref[...], wu_ref[...])
    gate_ref[...] = gate
    cw = cw_ref[...]
    y = cb_ref[...] + cw[0:1] * p0_ref[...] + cw[1:2] * p1_ref[...] + cw[2:3] * gate
    acc_ref[...] += _dot((_silu(y) * up).astype(BF16), wd_ref[...])

    @pl.when(j == pl.num_programs(0) - 1)
    def _():
        o_ref[...] = x_ref[...] + acc_ref[...]


def ffn_sample(x, prev, lw, *, tf):
    nb, d = x.shape
    nf = D_FF // tf
    li = lw['li']
    return pl.pallas_call(
        _ffn_sample_kernel,
        out_shape=(jax.ShapeDtypeStruct((nb, d), F32), jax.ShapeDtypeStruct((nb, D_FF), F32)),
        grid=(nf,),
        in_specs=[pl.BlockSpec((nb, d), lambda j: (0, 0)),
                  pl.BlockSpec((nb, tf), lambda j: (0, j)),
                  pl.BlockSpec((nb, tf), lambda j: (0, nf + j)),
                  pl.BlockSpec((1, d), lambda j: (0, 0)),
                  pl.BlockSpec((None, d, tf), lambda j: (li, 0, j)),
                  pl.BlockSpec((None, d, tf), lambda j: (li, 0, nf + j)),
                  pl.BlockSpec((FFN_CONV, tf), lambda j: (0, j)),
                  pl.BlockSpec((1, tf), lambda j: (0, j)),
                  pl.BlockSpec((None, tf, d), lambda j: (li, j, 0))],
        out_specs=(pl.BlockSpec((nb, d), lambda j: (0, 0)), pl.BlockSpec((nb, tf), lambda j: (0, j))),
        scratch_shapes=[pltpu.VMEM((nb, d), BF16), pltpu.VMEM((nb, d), F32)],
        compiler_params=_cparams(("arbitrary",)),
        name="ffn_sample",
    )(x, prev, prev, lw['g_ffn'], lw['w_gu'], lw['w_gu'], lw['ffn_conv_w'], lw['ffn_conv_b'], lw['w_down'])


def _rope_tables(pos):
    inv = ROPE_THETA ** (-jnp.arange(0, ROPE_DIM, 2, dtype=F32) / ROPE_DIM)
    ang = pos.astype(F32)[:, None] * inv[None, :]
    cos, sin = jnp.cos(ang), jnp.sin(ang)
    zero = jnp.zeros_like(cos)
    return (jnp.concatenate([cos, cos, zero, zero], axis=1),
            jnp.concatenate([-sin, zero, zero, zero], axis=1),
            jnp.concatenate([zero, sin, zero, zero], axis=1))


def _pack_stacked(p):
    w_in = p['w_in']
    depth, d, _ = w_in.shape
    offs = np.cumsum([0, Q_LORA, KV_LORA, ROPE_DIM, D_SSM, CONV_DIM, N_HEADS_S, D_CM, D_CM])
    seg = lambda k: w_in[:, :, int(offs[k]):int(offs[k + 1])]
    zpad = lambda n: jnp.zeros((depth, d, n), w_in.dtype)
    w_in_p = jnp.concatenate([seg(0), seg(3), seg(6), seg(7), seg(4), seg(1),
                              seg(2), zpad(LANE - ROPE_DIM), seg(5), zpad(LANE - N_HEADS_S)], axis=2).astype(BF16)
    return {'w_in': w_in_p, 'w_out': p['w_out'].astype(BF16), 'w_gu': p['w_gate_up'].astype(BF16),
            'w_down': p['w_down'].astype(BF16)}


def _pack_layer(li, p, stacked):
    h = N_HEADS_A
    w_uq = jnp.pad(p['w_uq'][li], ((0, 0), (0, 0), (0, QK_PAD - QK_DIM))).reshape(Q_LORA, h * QK_PAD).astype(BF16)
    w_uk = p['w_uk'][li].reshape(KV_LORA, h * NOPE_DIM).astype(BF16)
    w_uv = p['w_uv'][li].reshape(KV_LORA, h * V_DIM).astype(BF16)
    pad_g = lambda g: jnp.pad(g, (0, QK_PAD - QK_DIM))[None, :]
    row = lambda v: v[None, :]
    lane_pad = lambda v: jnp.pad(v, (0, LANE - v.shape[0]))[None, :]
    head_expand = np.zeros((LANE, D_SSM), np.float32)
    head_expand[:N_HEADS_S] = np.kron(np.eye(N_HEADS_S, dtype=np.float32), np.ones((1, SSM_HEAD_DIM), np.float32))
    head_expand = jnp.asarray(head_expand)
    return {
        'li': li, **stacked,
        'g_mix': row(p['g_mix'][li]),
        'g_q_a': row(p['g_q_a'][li]), 'g_kv_a': row(p['g_kv_a'][li]),
        'w_uq': w_uq, 'w_uk': w_uk, 'w_uk_t': w_uk.T, 'w_uv': w_uv,
        'gq': pad_g(p['g_qk_q'][li]) * (QK_DIM ** -0.5),
        'gk': pad_g(p['g_qk_k'][li]),
        'ssm_conv_w': p['ssm_conv_w'][li], 'ssm_conv_b': row(p['ssm_conv_b'][li]),
        'dt_bias': lane_pad(p['ssm_dt_bias'][li]), 'a_log': lane_pad(p['ssm_a_log'][li]),
        'd_skip': row(jnp.repeat(p['ssm_d'][li], SSM_HEAD_DIM)),
        'g_ssm_out': row(p['g_ssm_out'][li]), 'head_expand': head_expand,
        'g_sgu': row(p['g_sgu'][li]), 'w_spatial': p['w_spatial'][li], 'b_spatial_t': p['b_spatial'][li].T,
        'sgu_w0': row(jnp.repeat(p['w_spatial'][li][:, 0, 0], CG_DIM)),
        'sgu_b0': row(jnp.repeat(p['b_spatial'][li][:, 0], CG_DIM)),
        'g_ffn': row(p['g_ffn'][li]),
        'ffn_conv_w': p['ffn_conv_w'][li], 'ffn_conv_b': row(p['ffn_conv_b'][li]),
    }


def _tile(n, pref):
    t = min(n, pref)
    assert n % t == 0, (n, pref)
    return t


def kernel(x_prompt, x_sample, cache_latent, cache_krope, state_ssm, state_conv, state_ffn_conv, page_table, g_mix, w_in, g_q_a, w_uq, g_kv_a, w_uk, w_uv, g_qk_q, g_qk_k, ssm_conv_w, ssm_conv_b, ssm_dt_bias, ssm_a_log, ssm_d, g_ssm_out, g_sgu, w_spatial, b_spatial, w_out, g_ffn, w_gate_up, ffn_conv_w, ffn_conv_b, w_down):
    params = dict(g_mix=g_mix, w_in=w_in, g_q_a=g_q_a, w_uq=w_uq, g_kv_a=g_kv_a, w_uk=w_uk, w_uv=w_uv,
                  g_qk_q=g_qk_q, g_qk_k=g_qk_k, ssm_conv_w=ssm_conv_w, ssm_conv_b=ssm_conv_b,
                  ssm_dt_bias=ssm_dt_bias, ssm_a_log=ssm_a_log, ssm_d=ssm_d, g_ssm_out=g_ssm_out, g_sgu=g_sgu,
                  w_spatial=w_spatial, b_spatial=b_spatial, w_out=w_out, g_ffn=g_ffn, w_gate_up=w_gate_up,
                  ffn_conv_w=ffn_conv_w, ffn_conv_b=ffn_conv_b, w_down=w_down)
    depth = w_in.shape[0]
    bp, seq, d = x_prompt.shape
    bd, dec_seq, _ = x_sample.shape
    assert dec_seq == 1 and d == D_MODEL and cache_latent.shape[2] == PAGE
    n_pages = page_table.shape[1]
    past_len = n_pages * PAGE
    rp = bp * seq

    tabs_p = _rope_tables(jnp.arange(seq, dtype=jnp.int32))
    tabs_s = _rope_tables(jnp.full((bd,), past_len, jnp.int32))

    tm_p = _tile(seq, 512)
    tq = _tile(seq, 1024)
    tk = tq
    pp = next(c for c in (64, 32, 16, 8, 4, 2) if n_pages % c == 0)
    cache_krope_t = jnp.swapaxes(cache_krope, 2, 3)
    tf = 512

    yp = x_prompt.reshape(rp, d)
    ys = x_sample.reshape(bd, d)
    outs = [[] for _ in range(11)]
    stacked = _pack_stacked(params)
    for li in range(depth):
        lw = _pack_layer(li, params, stacked)

        proj = norm_matmul(yp, lw['g_mix'], lw['w_in'], li, tm=tm_p, tn=IN_PAD)
        q, k, v, lat, kro = mla_prep(proj, bp, seq, lw, lw['gq'] * LOG2E, tabs_p, tm=_tile(seq, 256), q_dtype=BF16)
        att = flash_attention(q, k, v, tq=tq, tk=tk)
        ssm, ssm_state = ssd_prompt(proj, bp, seq, lw)
        cm = spatial_gate(proj, lw, tm=tm_p)
        y_mid = out_proj(yp, att.reshape(rp, -1), ssm, cm, lw['w_out'], li, tm=tm_p, tn=d)
        yp = ffn_prompt(y_mid, seq, lw, tm=tm_p, tf=tf)
        tail = y_mid.reshape(bp, seq, d)[:, seq - (FFN_CONV - 1):, :].reshape(bp * (FFN_CONV - 1), d)
        tail = jnp.pad(tail, ((0, 8 - tail.shape[0]), (0, 0)))
        gate_tail = norm_matmul(tail, lw['g_ffn'], lw['w_gu'], li, tm=8, tn=tf, n=D_FF)[:bp * (FFN_CONV - 1)]
        proj3 = proj.reshape(bp, seq, IN_PAD)
        outs[0].append(lat)
        outs[1].append(kro)
        outs[2].append(ssm_state)
        outs[3].append(proj3[:, seq - (SSM_CONV - 1):, COL_XBC:COL_XBC + CONV_DIM])
        outs[4].append(gate_tail.reshape(bp, FFN_CONV - 1, D_FF))

        proj_s = norm_matmul(ys, lw['g_mix'], lw['w_in'], li, tm=bd, tn=IN_PAD)
        q_s, k_s, _, lat_s, kro_s = mla_prep(proj_s, 1, bd, lw, lw['gq'], tabs_s, tm=bd, q_dtype=F32)
        qabs, qgr = dec_q(q_s, lw['gk'], lw['w_uk'])
        m, l, acc = dec_paged(li, page_table, cache_latent, cache_krope_t, lw['w_uk_t'],
                              qabs.transpose(1, 0, 2), qgr.transpose(1, 0, 2), pp=pp)
        att_s = dec_finish(m.transpose(1, 0, 2), l.transpose(1, 0, 2), acc.transpose(1, 0, 2), q_s, k_s,
                           lat_s[0], lw['w_uv'])
        conv_prev = state_conv[li]
        x_s, bc_s, xdt_t, dec_t = ssd_sample_prep(proj_s, conv_prev.reshape(bd, -1), lw)
        ssm_state_s, y_t = ssd_sample_state(li, state_ssm, xdt_t, dec_t, bc_s)
        ssm_s, cm_s, vn_s = sample_mix(y_t, x_s, proj_s, lw)
        ys_mid = out_proj(ys, att_s, ssm_s, cm_s, lw['w_out'], li, tm=bd, tn=d)
        ffn_prev = state_ffn_conv[li]
        ys, gate_s = ffn_sample(ys_mid, ffn_prev.reshape(bd, -1), lw, tf=tf)
        outs[5].append(lat_s.reshape(bd, 1, KV_LORA))
        outs[6].append(kro_s.reshape(bd, 1, ROPE_DIM))
        outs[7].append(ssm_state_s)
        outs[8].append(jnp.concatenate([conv_prev[:, 1:], proj_s[:, None, COL_XBC:COL_XBC + CONV_DIM]], axis=1))
        outs[9].append(jnp.concatenate([ffn_prev[:, 1:], gate_s[:, None, :]], axis=1))
        outs[10].append(vn_s.reshape(bd, 1, D_CM))

    stacked = [jnp.stack(o) for o in outs]
    return (yp.reshape(bp, seq, d), ys.reshape(bd, 1, d), *stacked)
```
